```python
import jax, jax.numpy as jnp
from jax import lax
import numpy as np

D_MODEL = 1024
BATCH = 1
SEQ = 16384
DEPTH = 4

N_A_LAYERS = DEPTH // 2
N_B_LAYERS = DEPTH - N_A_LAYERS
D_FF = 4 * D_MODEL
NORM_EPS = 1e-6
NEG_INF = -1e30
GATE_FLOOR = 1e-30

HG_EXPAND = 128
HG_HEADS = D_MODEL // HG_EXPAND
HG_DK = HG_EXPAND
HG_DV = D_MODEL // HG_HEADS
HG_CHUNK = 64

NSA_HEADS = 16
NSA_KV_HEADS = 4
NSA_GROUP = NSA_HEADS // NSA_KV_HEADS
NSA_DH = D_MODEL // NSA_HEADS
CMP_BLOCK = 32
CMP_STRIDE = 16
CMP_HIDDEN = 4 * NSA_DH
SLC_BLOCK = 64
SLC_TOPK = 16
WINDOW = 512
Q_BLOCK = 128
FORCE_BONUS = 1e4
N_KV_STREAMS = 6

kernel_name = 'hgrn2_nsa_yoco_hybrid'


def rmsnorm(x, w):
    xf = x.astype(jnp.float32)
    y = xf * lax.rsqrt(jnp.mean(jnp.square(xf), axis=-1, keepdims=True) + NORM_EPS)
    return (y * w.astype(jnp.float32)).astype(x.dtype)


def squared_relu_mlp(h, w_up, w_down):
    return jnp.square(jax.nn.relu(h @ w_up)) @ w_down


def hgrn2_lower_bounds(lb_param):
    p = jax.nn.softmax(lb_param.astype(jnp.float32), axis=0)
    return jnp.cumsum(p, axis=0) - p[0:1]


def alibi_slopes(n):
    return jnp.exp2(-8.0 * jnp.arange(1, n + 1, dtype=jnp.float32) / n)


def masked_softmax(s, mask):
    p = jax.nn.softmax(jnp.where(mask, s, NEG_INF), axis=-1)
    return jnp.where(mask, p, 0.0)


def hgrn2_mixer(h, w_in, gnorm_w, w_out, lb):
    B, T, _ = h.shape
    hk = HG_HEADS * HG_DK
    hv = HG_HEADS * HG_DV
    proj = h @ w_in
    q = proj[..., :hk]
    f_pre = proj[..., hk:2 * hk].astype(jnp.float32)
    v_in = proj[..., 2 * hk:2 * hk + hv]
    g_out = proj[..., 2 * hk + hv:]
    f_gate = lb + (1.0 - lb) * jax.nn.sigmoid(f_pre)
    log_f = jnp.log(jnp.maximum(f_gate, GATE_FLOOR))
    k_in = (1.0 - lb) * jax.nn.sigmoid(-f_pre)
    nc = T // HG_CHUNK

    def chunks(t, d):
        return t.astype(jnp.float32).reshape(B, nc, HG_CHUNK, HG_HEADS, d).transpose(1, 0, 3, 2, 4)

    qc, kc, vc, gc = chunks(q, HG_DK), chunks(k_in, HG_DK), chunks(v_in, HG_DV), chunks(log_f, HG_DK)
    causal = jnp.tril(jnp.ones((HG_CHUNK, HG_CHUNK), dtype=bool))[None, None, :, :, None]

    def step(S, inp):
        qb, kb, vb, gb = inp
        bcum = jnp.cumsum(gb, axis=2)
        diff = bcum[:, :, :, None, :] - bcum[:, :, None, :, :]
        decay = jnp.exp(jnp.where(causal, diff, NEG_INF))
        attn = jnp.einsum('bhtd,bhsd,bhtsd->bhts', qb, kb, decay)
        o = (jnp.einsum('bhts,bhsv->bhtv', attn, vb)
             + jnp.einsum('bhtd,bhdv->bhtv', qb * jnp.exp(bcum), S))
        blast = bcum[:, :, -1:, :]
        S = (jnp.exp(blast[:, :, 0, :])[..., None] * S
             + jnp.einsum('bhsd,bhsv->bhdv', kb * jnp.exp(blast - bcum), vb))
        return S, o

    S0 = jnp.zeros((B, HG_HEADS, HG_DK, HG_DV), jnp.float32)
    _, o = lax.scan(step, S0, (qc, kc, vc, gc))
    o = o.transpose(1, 0, 3, 2, 4).reshape(B, T, HG_HEADS, HG_DV)
    o = rmsnorm(o, gnorm_w).reshape(B, T, hv) * jax.nn.silu(g_out.astype(jnp.float32))
    return o.astype(h.dtype) @ w_out


def compress_blocks(x, pe, w1, w2):
    B, G, T, dh = x.shape
    nc = T // CMP_STRIDE
    xp = jnp.pad(x, ((0, 0), (0, 0), (0, CMP_BLOCK - CMP_STRIDE), (0, 0)))
    parts = [xp[:, :, j * CMP_STRIDE:j * CMP_STRIDE + T].reshape(B, G, nc, CMP_STRIDE, dh)
             for j in range(CMP_BLOCK // CMP_STRIDE)]
    blocks = jnp.concatenate(parts, axis=3) + pe
    hidden = jax.nn.gelu(blocks.reshape(B, G, nc, CMP_BLOCK * dh) @ w1)
    return hidden @ w2


def nsa_shared_kv(h, kv_w, pe_k, w1_k, w2_k, pe_v, w1_v, w2_v):
    B, T, _ = h.shape
    G, dh = NSA_KV_HEADS, NSA_DH
    kv = (h @ kv_w).reshape(B, T, N_KV_STREAMS, G, dh).transpose(2, 0, 3, 1, 4)
    k_cmp = compress_blocks(kv[0], pe_k, w1_k, w2_k)
    v_cmp = compress_blocks(kv[1], pe_v, w1_v, w2_v)
    n_slc = T // SLC_BLOCK
    k_slc = kv[2].reshape(B, G, n_slc, SLC_BLOCK, dh)
    v_slc = kv[3].reshape(B, G, n_slc, SLC_BLOCK, dh)
    pad = ((0, 0), (0, 0), (WINDOW, 0), (0, 0))
    k_win = jnp.pad(kv[4], pad)
    v_win = jnp.pad(kv[5], pad)
    return k_cmp, v_cmp, k_slc, v_slc, k_win, v_win


def nsa_mixer(h, w_in, w_out, k_cmp, v_cmp, k_slc, v_slc, k_win, v_win):
    B, T, _ = h.shape
    G, J, dh = NSA_KV_HEADS, NSA_GROUP, NSA_DH
    nq = T // Q_BLOCK
    n_cmp = k_cmp.shape[2]
    n_slc = k_slc.shape[2]
    top_k = min(SLC_TOPK, n_slc)
    scale = dh ** -0.5
    slopes = alibi_slopes(NSA_HEADS).reshape(1, G, J, 1, 1)
    proj = h @ w_in
    q = proj[..., :NSA_HEADS * dh].reshape(B, nq, Q_BLOCK, G, J, dh).transpose(1, 0, 3, 4, 2, 5)
    gates = jax.nn.sigmoid(proj[..., NSA_HEADS * dh:].astype(jnp.float32))
    gates = gates.reshape(B, nq, Q_BLOCK, G, J, 3).transpose(1, 0, 3, 4, 2, 5)
    cmp_end = jnp.arange(n_cmp) * CMP_STRIDE + (CMP_BLOCK - 1)
    slc_idx = jnp.arange(n_slc)
    tok_in_blk = jnp.arange(SLC_BLOCK)
    win_off = jnp.arange(WINDOW + Q_BLOCK)
    bi = jnp.arange(B)[:, None, None, None]
    gi = jnp.arange(G)[None, :, None, None]
    r = SLC_BLOCK // CMP_STRIDE
    left = CMP_BLOCK // CMP_STRIDE - 1

    def block(args):
        qi, qb, gb = args
        t_pos = qi * Q_BLOCK + jnp.arange(Q_BLOCK)
        dist_c = t_pos[:, None] - cmp_end[None, :]
        s_c = (jnp.einsum('bgjqd,bgcd->bgjqc', qb, k_cmp).astype(jnp.float32) * scale
               - slopes * dist_c.astype(jnp.float32))
        p_cmp = masked_softmax(s_c, dist_c >= 0)
        o_cmp = jnp.einsum('bgjqc,bgcd->bgjqd', p_cmp.astype(v_cmp.dtype), v_cmp)
        imp = jnp.pad(p_cmp.sum(axis=2), ((0, 0), (0, 0), (0, 0), (left, r)))
        imp_slc = imp[..., 0:r * n_slc:r]
        for u in range(1, r + left):
            imp_slc = imp_slc + imp[..., u:u + r * n_slc:r]
        cur = t_pos // SLC_BLOCK
        blk_ok = slc_idx[None, :] <= cur[:, None]
        forced = ((slc_idx[None, :] == 0) | (slc_idx[None, :] == cur[:, None])
                  | (slc_idx[None, :] == cur[:, None] - 1))
        score = jnp.where(blk_ok, imp_slc + FORCE_BONUS * forced, NEG_INF)
        top_s, top_i = lax.top_k(score, top_k)
        k_sel = k_slc[bi, gi, top_i].reshape(B, G, Q_BLOCK, top_k * SLC_BLOCK, dh)
        v_sel = v_slc[bi, gi, top_i].reshape(B, G, Q_BLOCK, top_k * SLC_BLOCK, dh)
        pos5 = top_i[..., None] * SLC_BLOCK + tok_in_blk
        dist5 = t_pos[None, None, :, None, None] - pos5
        ok5 = (top_s > NEG_INF * 0.5)[..., None] & (dist5 >= 0)
        dist_s = dist5.reshape(B, G, Q_BLOCK, top_k * SLC_BLOCK)[:, :, None]
        ok_s = ok5.reshape(B, G, Q_BLOCK, top_k * SLC_BLOCK)[:, :, None]
        s_s = (jnp.einsum('bgjqd,bgqkd->bgjqk', qb, k_sel).astype(jnp.float32) * scale
               - slopes * dist_s.astype(jnp.float32))
        p_s = masked_softmax(s_s, ok_s)
        o_sel = jnp.einsum('bgjqk,bgqkd->bgjqd', p_s.astype(v_sel.dtype), v_sel)
        kw = lax.dynamic_slice_in_dim(k_win, qi * Q_BLOCK, WINDOW + Q_BLOCK, axis=2)
        vw = lax.dynamic_slice_in_dim(v_win, qi * Q_BLOCK, WINDOW + Q_BLOCK, axis=2)
        pos_w = qi * Q_BLOCK - WINDOW + win_off
        dist_w = t_pos[:, None] - pos_w[None, :]
        ok_w = (pos_w[None, :] >= 0) & (dist_w >= 0) & (dist_w < WINDOW)
        s_w = (jnp.einsum('bgjqd,bgkd->bgjqk', qb, kw).astype(jnp.float32) * scale
               - slopes * dist_w.astype(jnp.float32))
        p_w = masked_softmax(s_w, ok_w)
        o_win = jnp.einsum('bgjqk,bgkd->bgjqd', p_w.astype(vw.dtype), vw)
        out = (gb[..., 0:1] * o_cmp.astype(jnp.float32) + gb[..., 1:2] * o_sel.astype(jnp.float32)
               + gb[..., 2:3] * o_win.astype(jnp.float32))
        return out.astype(h.dtype)

    outs = lax.map(block, (jnp.arange(nq), q, gates))
    o = outs.transpose(1, 0, 4, 2, 3, 5).reshape(B, T, NSA_HEADS * dh)
    return o @ w_out


def setup_inputs(seed: int = 0) -> dict:
    key = jax.random.key(seed)
    ks = jax.random.split(key, 24)
    f32 = jnp.float32

    def nrm(k, shape, fan_in):
        return jax.random.normal(k, shape, f32) * fan_in ** -0.5

    def gain(k, shape):
        return 1.0 + 0.01 * jax.random.normal(k, shape, f32)

    hk = HG_HEADS * HG_DK
    hv = HG_HEADS * HG_DV
    a_in_width = 2 * hk + 2 * hv
    b_q_width = NSA_HEADS * NSA_DH
    b_in_width = b_q_width + 3 * NSA_HEADS
    kv_width = N_KV_STREAMS * NSA_KV_HEADS * NSA_DH
    cmp_in = CMP_BLOCK * NSA_DH
    return {
        'x': jax.random.normal(ks[0], (BATCH, SEQ, D_MODEL), f32),
        'a_norm_w': gain(ks[1], (N_A_LAYERS, D_MODEL)),
        'a_w_in': nrm(ks[2], (N_A_LAYERS, D_MODEL, a_in_width), D_MODEL),
        'a_gnorm_w': gain(ks[3], (N_A_LAYERS, HG_DV)),
        'a_w_out': nrm(ks[4], (N_A_LAYERS, hv, D_MODEL), hv),
        'a_lower_bounds': 0.5 * jax.random.normal(ks[5], (N_A_LAYERS, hk), f32),
        'kv_norm_w': gain(ks[6], (D_MODEL,)),
        'kv_w': nrm(ks[7], (D_MODEL, kv_width), D_MODEL),
        'cmp_pe_k': 0.1 * jax.random.normal(ks[8], (CMP_BLOCK, NSA_DH), f32),
        'cmp_w1_k': nrm(ks[9], (cmp_in, CMP_HIDDEN), cmp_in),
        'cmp_w2_k': nrm(ks[10], (CMP_HIDDEN, NSA_DH), CMP_HIDDEN),
        'cmp_pe_v': 0.1 * jax.random.normal(ks[11], (CMP_BLOCK, NSA_DH), f32),
        'cmp_w1_v': nrm(ks[12], (cmp_in, CMP_HIDDEN), cmp_in),
        'cmp_w2_v': nrm(ks[13], (CMP_HIDDEN, NSA_DH), CMP_HIDDEN),
        'b_norm_w': gain(ks[14], (N_B_LAYERS, D_MODEL)),
        'b_w_in': nrm(ks[15], (N_B_LAYERS, D_MODEL, b_in_width), D_MODEL),
        'b_w_out': nrm(ks[16], (N_B_LAYERS, b_q_width, D_MODEL), b_q_width),
        'mlp_norm_w': gain(ks[17], (DEPTH, D_MODEL)),
        'mlp_w_up': nrm(ks[18], (DEPTH, D_MODEL, D_FF), D_MODEL),
        'mlp_w_down': nrm(ks[19], (DEPTH, D_FF, D_MODEL), D_FF),
        'final_norm_w': gain(ks[20], (D_MODEL,)),
    }


def reference(x, a_norm_w, a_w_in, a_gnorm_w, a_w_out, a_lower_bounds, kv_norm_w, kv_w,
              cmp_pe_k, cmp_w1_k, cmp_w2_k, cmp_pe_v, cmp_w1_v, cmp_w2_v,
              b_norm_w, b_w_in, b_w_out, mlp_norm_w, mlp_w_up, mlp_w_down, final_norm_w):
    lbs = hgrn2_lower_bounds(a_lower_bounds)
    shared = None
    for layer in range(DEPTH):
        if layer < N_A_LAYERS:
            x = x + hgrn2_mixer(rmsnorm(x, a_norm_w[layer]), a_w_in[layer], a_gnorm_w[layer],
                                a_w_out[layer], lbs[layer])
        else:
            if shared is None:
                shared = nsa_shared_kv(rmsnorm(x, kv_norm_w), kv_w, cmp_pe_k, cmp_w1_k, cmp_w2_k,
                                       cmp_pe_v, cmp_w1_v, cmp_w2_v)
            b = layer - N_A_LAYERS
            x = x + nsa_mixer(rmsnorm(x, b_norm_w[b]), b_w_in[b], b_w_out[b], *shared)
        x = x + squared_relu_mlp(rmsnorm(x, mlp_norm_w[layer]), mlp_w_up[layer], mlp_w_down[layer])
    return rmsnorm(x, final_norm_w)
```

```python
import functools

import jax
import jax.numpy as jnp
from jax import lax
from jax.experimental import pallas as pl
from jax.experimental.pallas import tpu as pltpu

F32 = jnp.float32
BF16 = jnp.bfloat16

NORM_EPS = 1e-6
NEG_INF = -1e30
GATE_FLOOR = 1e-30

HG_HEADS = 8
HG_DK = 128
HG_DV = 128
HG_CHUNK = 64

NSA_HEADS = 16
NSA_KV_HEADS = 4
NSA_GROUP = NSA_HEADS // NSA_KV_HEADS
NSA_DH = 64
CMP_BLOCK = 32
CMP_STRIDE = 16
SLC_BLOCK = 64
SLC_TOPK = 16
WINDOW = 512
Q_BLOCK = 128
FORCE_BONUS = 1e4
N_KV_STREAMS = 6

KEY_TILE = 128
WORD_BITS = 16
QL = NSA_GROUP * Q_BLOCK

VMEM_LIMIT = 56 * 1024 * 1024


def _params(n_axes, vmem=VMEM_LIMIT):
    return pltpu.CompilerParams(dimension_semantics=("arbitrary",) * n_axes, vmem_limit_bytes=vmem)


def _nt(a, b):
    return lax.dot_general(a, b, (((1,), (1,)), ((), ())), preferred_element_type=F32)


def _tn(a, b):
    return lax.dot_general(a, b, (((0,), (0,)), ((), ())), preferred_element_type=F32)


def _dot(a, b):
    return jnp.dot(a, b, preferred_element_type=F32)


def _rmsnorm(x, w):
    ms = jnp.mean(x * x, axis=-1, keepdims=True)
    return x * lax.rsqrt(ms + NORM_EPS) * w


def _norm_matmul_kernel(x_ref, nw_ref, w_ref, o_ref):
    h = _rmsnorm(x_ref[...], nw_ref[...]).astype(BF16)
    o_ref[...] = _dot(h, w_ref[...])


def _norm_matmul(x, nw, w, tm=256):
    t, d = x.shape
    n = w.shape[1]
    return pl.pallas_call(
        _norm_matmul_kernel,
        grid=(t // tm,),
        in_specs=[pl.BlockSpec((tm, d), lambda i: (i, 0)),
                  pl.BlockSpec((1, d), lambda i: (0, 0)),
                  pl.BlockSpec((d, n), lambda i: (0, 0))],
        out_specs=pl.BlockSpec((tm, n), lambda i: (i, 0)),
        out_shape=jax.ShapeDtypeStruct((t, n), F32),
        compiler_params=_params(1),
        name="norm_matmul",
    )(x, nw.reshape(1, d), w)


def _hgrn_kernel(proj_ref, lbp_ref, gw_ref, o_ref, st_ref, *, layer):
    c = HG_CHUNK
    hk = HG_HEADS * HG_DK

    @pl.when(pl.program_id(0) == 0)
    def _():
        st_ref[...] = jnp.zeros_like(st_ref)

    q = proj_ref[:, 0:hk]
    fp = proj_ref[:, hk:2 * hk]
    v = proj_ref[:, 2 * hk:3 * hk]
    go = proj_ref[:, 3 * hk:4 * hk]

    a = lbp_ref[...]
    e = jnp.exp(a - jnp.max(a, axis=0, keepdims=True))
    p = e / jnp.sum(e, axis=0, keepdims=True)
    cum = p[0:1]
    for i in range(1, layer + 1):
        cum = cum + p[i:i + 1]
    lb = cum - p[0:1]

    ea = jnp.exp(-jnp.abs(fp))
    r = 1.0 / (1.0 + ea)
    pos = fp >= 0
    one_m = 1.0 - lb
    f_gate = lb + one_m * jnp.where(pos, r, ea * r)
    kk = one_m * jnp.where(pos, ea * r, r)
    b = jnp.log(jnp.maximum(f_gate, GATE_FLOOR))

    row = lax.broadcasted_iota(jnp.int32, (c, hk), 0)
    for j in range(6):
        sh = 1 << j
        b = b + jnp.where(row >= sh, pltpu.roll(b, sh, 0), 0.0)
    blast = b[c - 1:c, :]

    xs = []
    z = b
    for j in range(6):
        m = 1 << j
        upper = ((row >> j) & 1) == 1
        y = jnp.where(upper, z, pltpu.roll(z, c - m, 0))
        ex = jnp.exp(jnp.where(upper, b - y, y - b))
        xs.append((jnp.where(upper, q, kk) * ex).astype(BF16))
        if j < 5:
            z = jnp.where(upper, pltpu.roll(z, m, 0), z)

    ti = lax.broadcasted_iota(jnp.int32, (c, c), 0)
    si = lax.broadcasted_iota(jnp.int32, (c, c), 1)
    masks = [(((ti >> j) & 1) == 1) & (((si >> j) & 1) == 0) & ((ti >> (j + 1)) == (si >> (j + 1)))
             for j in range(6)]
    diag = ti == si

    qb = q.astype(BF16)
    kb = kk.astype(BF16)
    vb = v.astype(BF16)
    qs = (q * jnp.exp(b)).astype(BF16)
    khat = (kk * jnp.exp(blast - b)).astype(BF16)
    eb = jnp.exp(blast)
    gw = gw_ref[...]

    for h in range(HG_HEADS):
        sl = slice(h * HG_DK, (h + 1) * HG_DK)
        attn = jnp.where(diag, _nt(qb[:, sl], kb[:, sl]), 0.0)
        for j in range(6):
            xj = xs[j][:, sl]
            attn = attn + jnp.where(masks[j], _nt(xj, xj), 0.0)
        st = st_ref[h]
        o = _dot(attn.astype(BF16), vb[:, sl]) + _nt(qs[:, sl], st.astype(BF16))
        st_ref[h] = st * eb[:, sl] + _tn(vb[:, sl], khat[:, sl])
        on = _rmsnorm(o, gw)
        g = go[:, sl]
        o_ref[:, sl] = (on * (g * (1.0 / (1.0 + jnp.exp(-g))))).astype(o_ref.dtype)


def _hgrn(proj, lb_param, gnorm_w, layer):
    t = proj.shape[0]
    hk = HG_HEADS * HG_DK
    hv = HG_HEADS * HG_DV
    n_layers = lb_param.shape[0]
    return pl.pallas_call(
        functools.partial(_hgrn_kernel, layer=layer),
        grid=(t // HG_CHUNK,),
        in_specs=[pl.BlockSpec((HG_CHUNK, 2 * hk + 2 * hv), lambda i: (i, 0)),
                  pl.BlockSpec((n_layers, hk), lambda i: (0, 0)),
                  pl.BlockSpec((1, HG_DV), lambda i: (0, 0))],
        out_specs=pl.BlockSpec((HG_CHUNK, hv), lambda i: (i, 0)),
        out_shape=jax.ShapeDtypeStruct((t, hv), BF16),
        scratch_shapes=[pltpu.VMEM((HG_HEADS, HG_DV, HG_DK), F32)],
        compiler_params=_params(1),
        name="hgrn",
    )(proj, lb_param, gnorm_w.reshape(1, HG_DV))


def _proj_mlp_kernel(x_ref, a_ref, wo_ref, nw_ref, wup_ref, wdn_ref, fnw_ref, o_ref, *, final, ff_tile):
    x1 = x_ref[...] + _dot(a_ref[...], wo_ref[...])
    h = _rmsnorm(x1, nw_ref[...]).astype(BF16)
    acc = x1
    d_ff = wup_ref.shape[1]
    for c0 in range(0, d_ff, ff_tile):
        u = jnp.maximum(_dot(h, wup_ref[:, c0:c0 + ff_tile]), 0.0)
        acc = acc + _dot((u * u).astype(BF16), wdn_ref[c0:c0 + ff_tile, :])
    if final:
        acc = _rmsnorm(acc, fnw_ref[...])
    o_ref[...] = acc


def _proj_mlp(x, a, wo, nw, wup, wdn, fnw, final, tm=256):
    t, d = x.shape
    d_ff = wup.shape[1]
    const = lambda i: (0, 0)
    return pl.pallas_call(
        functools.partial(_proj_mlp_kernel, final=final, ff_tile=min(1024, d_ff)),
        grid=(t // tm,),
        in_specs=[pl.BlockSpec((tm, d), lambda i: (i, 0)),
                  pl.BlockSpec((tm, a.shape[1]), lambda i: (i, 0)),
                  pl.BlockSpec(wo.shape, const),
                  pl.BlockSpec((1, d), const),
                  pl.BlockSpec(wup.shape, const),
                  pl.BlockSpec(wdn.shape, const),
                  pl.BlockSpec((1, d), const)],
        out_specs=pl.BlockSpec((tm, d), lambda i: (i, 0)),
        out_shape=jax.ShapeDtypeStruct((t, d), F32),
        compiler_params=_params(1),
        name="proj_mlp",
    )(x, a, wo, nw.reshape(1, d), wup, wdn, fnw.reshape(1, d))


def _kv_proj_kernel(x_ref, nw_ref, wn_ref, wt_ref, on_ref, ot_ref):
    h = _rmsnorm(x_ref[...], nw_ref[...]).astype(BF16)
    on_ref[...] = _dot(h, wn_ref[...])
    ot_ref[...] = _nt(wt_ref[...], h).astype(BF16)


def _kv_proj(x, nw, wn, wt, tm=256):
    t, d = x.shape
    const = lambda i: (0, 0)
    return pl.pallas_call(
        _kv_proj_kernel,
        grid=(t // tm,),
        in_specs=[pl.BlockSpec((tm, d), lambda i: (i, 0)),
                  pl.BlockSpec((1, d), const),
                  pl.BlockSpec(wn.shape, const),
                  pl.BlockSpec(wt.shape, const)],
        out_specs=[pl.BlockSpec((tm, wn.shape[1]), lambda i: (i, 0)),
                   pl.BlockSpec((wt.shape[0], tm), lambda i: (0, i))],
        out_shape=[jax.ShapeDtypeStruct((t, wn.shape[1]), F32),
                   jax.ShapeDtypeStruct((wt.shape[0], t), BF16)],
        compiler_params=_params(1),
        name="kv_proj",
    )(x, nw.reshape(1, d), wn, wt)


def _compress_kernel(hf_ref, pe_ref, w1_ref, w2_ref, o_ref):
    x = hf_ref[0, 0]
    nc, half = x.shape
    pe_lo = pe_ref[0, 0:1, :]
    pe_hi = pe_ref[0, 1:2, :]
    w1_lo = w1_ref[0, 0:half, :]
    w1_hi = w1_ref[0, half:2 * half, :]
    u = _dot((x + pe_lo).astype(BF16), w1_lo)
    vv = _dot((x + pe_hi).astype(BF16), w1_hi)
    nxt = pltpu.roll(vv, nc - 1, 0)
    pad = _dot(jnp.broadcast_to(pe_hi, (8, half)).astype(BF16), w1_hi)[0:1]
    row = lax.broadcasted_iota(jnp.int32, vv.shape, 0)
    pre = u + jnp.where(row == nc - 1, pad, nxt)
    hid = 0.5 * pre * (1.0 + jnp.tanh(0.7978845608028654 * (pre + 0.044715 * (pre * pre * pre))))
    o_ref[0, 0] = _dot(hid.astype(BF16), w2_ref[0])


def _compress(hf, pe, w1, w2):
    s, g, nc, half = hf.shape
    return pl.pallas_call(
        _compress_kernel,
        grid=(s, g),
        in_specs=[pl.BlockSpec((1, 1, nc, half), lambda i, j: (i, j, 0, 0)),
                  pl.BlockSpec((1, 2, half), lambda i, j: (i, 0, 0)),
                  pl.BlockSpec((1,) + w1.shape[1:], lambda i, j: (i, 0, 0)),
                  pl.BlockSpec((1,) + w2.shape[1:], lambda i, j: (i, 0, 0))],
        out_specs=pl.BlockSpec((1, 1, nc, w2.shape[2]), lambda i, j: (i, j, 0, 0)),
        out_shape=jax.ShapeDtypeStruct((s, g, nc, w2.shape[2]), F32),
        compiler_params=_params(2),
        name="compress",
    )(hf, pe, w1, w2)


def _nsa_cmp_kernel(q_ref, kc_ref, vct_ref, sl_ref, oc_ref, sel_ref, w_ref, p_ref, *, ns, topk):
    qi = pl.program_id(1)
    t0f = (qi * Q_BLOCK).astype(F32)
    qt = q_ref[0, 0]
    slope = sl_ref[0]
    lane = lax.broadcasted_iota(jnp.int32, (ns, QL), 1)
    rowi = lax.broadcasted_iota(jnp.int32, (ns, QL), 0)
    ratio = SLC_BLOCK // CMP_STRIDE
    base = ((lane & (Q_BLOCK - 1)) - SLC_BLOCK * rowi).astype(F32)

    mx = jnp.full((1, QL), NEG_INF, F32)
    for u in range(ratio):
        s = _dot(kc_ref[0, u * ns:(u + 1) * ns, :], qt)
        dist = base + (t0f - float(CMP_STRIDE * u + CMP_BLOCK - 1))
        s = jnp.where(dist >= 0, s - slope * dist, NEG_INF)
        p_ref[u * ns:(u + 1) * ns, :] = s
        mx = jnp.maximum(mx, jnp.max(s, axis=0, keepdims=True))
    valid = mx > 0.5 * NEG_INF
    lsum = jnp.zeros((1, QL), F32)
    for u in range(ratio):
        p = jnp.where(valid, jnp.exp(p_ref[u * ns:(u + 1) * ns, :] - mx), 0.0)
        p_ref[u * ns:(u + 1) * ns, :] = p
        lsum = lsum + jnp.sum(p, axis=0, keepdims=True)
    inv = jnp.where(lsum > 0, 1.0 / lsum, 0.0)
    acc = jnp.zeros((NSA_DH, QL), F32)
    imp = []
    for u in range(ratio):
        p = p_ref[u * ns:(u + 1) * ns, :]
        acc = acc + _dot(vct_ref[0, :, u * ns:(u + 1) * ns], p.astype(BF16))
        pn = p * inv
        tot = pn[:, 0:Q_BLOCK]
        for j in range(1, NSA_GROUP):
            tot = tot + pn[:, j * Q_BLOCK:(j + 1) * Q_BLOCK]
        imp.append(tot)
    oc_ref[0, 0] = acc * inv

    n_i = lax.broadcasted_iota(jnp.int32, (ns, Q_BLOCK), 0)
    ql = lax.broadcasted_iota(jnp.int32, (ns, Q_BLOCK), 1)
    pooled = jnp.where(n_i >= 1, pltpu.roll(imp[ratio - 1], 1, 0), 0.0)
    for u in range(ratio):
        pooled = pooled + imp[u]
    cur = 2 * qi + (ql >= SLC_BLOCK).astype(jnp.int32)
    forced = (n_i == 0) | (n_i == cur) | (n_i == cur - 1)
    score = jnp.where(n_i <= cur, pooled + jnp.where(forced, FORCE_BONUS, 0.0), NEG_INF)

    n_f = n_i.astype(F32)
    sel = jnp.zeros((ns, Q_BLOCK), F32)
    for _ in range(topk):
        best = jnp.max(score, axis=0, keepdims=True)
        idx = jnp.min(jnp.where(score == best, n_f, float(ns)), axis=0, keepdims=True)
        pick = n_f == idx
        sel = jnp.where(pick & (best > 0.5 * NEG_INF), 1.0, sel)
        score = jnp.where(pick, -jnp.inf, score)
    sel_ref[0, 0] = sel

    any_q = jnp.max(sel, axis=1, keepdims=True)
    wk = lax.broadcasted_iota(jnp.int32, (ns, 128), 1)
    wn = lax.broadcasted_iota(jnp.int32, (ns, 128), 0)
    weight = jnp.where((wn // WORD_BITS) == wk,
                       jnp.left_shift(1, wn & (WORD_BITS - 1)).astype(F32), 0.0)
    w_ref[0, 0] = jnp.sum(any_q * weight, axis=0, keepdims=True).astype(jnp.int32)


def _nsa_cmp(qt, kc, vct, slopes):
    g, nq, dh, _ = qt.shape
    ncmp = kc.shape[1]
    ns = ncmp // (SLC_BLOCK // CMP_STRIDE)
    topk = min(SLC_TOPK, ns)
    return pl.pallas_call(
        functools.partial(_nsa_cmp_kernel, ns=ns, topk=topk),
        grid=(g, nq),
        in_specs=[pl.BlockSpec((1, 1, dh, QL), lambda i, j: (i, j, 0, 0)),
                  pl.BlockSpec((1, ncmp, dh), lambda i, j: (i, 0, 0)),
                  pl.BlockSpec((1, dh, ncmp), lambda i, j: (i, 0, 0)),
                  pl.BlockSpec((1, 1, QL), lambda i, j: (i, 0, 0))],
        out_specs=[pl.BlockSpec((1, 1, dh, QL), lambda i, j: (i, j, 0, 0)),
                   pl.BlockSpec((1, 1, ns, Q_BLOCK), lambda i, j: (i, j, 0, 0)),
                   pl.BlockSpec((1, 1, 1, 128), lambda i, j: (i, j, 0, 0))],
        out_shape=[jax.ShapeDtypeStruct((g, nq, dh, QL), F32),
                   jax.ShapeDtypeStruct((g, nq, ns, Q_BLOCK), F32),
                   jax.ShapeDtypeStruct((g, nq, 1, 128), jnp.int32)],
        scratch_shapes=[pltpu.VMEM((ncmp, QL), F32)],
        compiler_params=_params(2),
        name="nsa_cmp",
    )(qt, kc, vct, slopes)


def _nsa_sw_kernel(words_ref, q_ref, gl_ref, sl_ref, oc_ref, sel_ref, ks_ref, vst_ref, kw_ref, vwt_ref,
                   o_ref, m_ref, l_ref, acc_ref, *, nq, nwords):
    g = pl.program_id(0)
    qi = pl.program_id(1)
    t0 = qi * Q_BLOCK
    qt = q_ref[0, 0]
    slope = sl_ref[0]
    lane = lax.broadcasted_iota(jnp.int32, (KEY_TILE, QL), 1)
    rowi = lax.broadcasted_iota(jnp.int32, (KEY_TILE, QL), 0)
    base = ((lane & (Q_BLOCK - 1)) - rowi).astype(F32)

    def init():
        m_ref[...] = jnp.full(m_ref.shape, NEG_INF, F32)
        l_ref[...] = jnp.zeros(l_ref.shape, F32)
        acc_ref[...] = jnp.zeros(acc_ref.shape, F32)

    def update(kblk, vtblk, dist, mask):
        s = jnp.where(mask, _dot(kblk, qt) - slope * dist, NEG_INF)
        m_old = m_ref[...]
        m_new = jnp.maximum(m_old, jnp.max(s, axis=0, keepdims=True))
        alpha = jnp.exp(m_old - m_new)
        p = jnp.where(mask, jnp.exp(s - m_new), 0.0)
        l_ref[...] = alpha * l_ref[...] + jnp.sum(p, axis=0, keepdims=True)
        acc_ref[...] = alpha * acc_ref[...] + _dot(vtblk, p.astype(BF16))
        m_ref[...] = m_new

    def result():
        lsum = l_ref[...]
        return acc_ref[...] * jnp.where(lsum > 0, 1.0 / lsum, 0.0)

    init()
    start = jnp.maximum(t0 - WINDOW, 0)
    for u in range((WINDOW + Q_BLOCK) // KEY_TILE):
        pos0 = pl.multiple_of(start + KEY_TILE * u, KEY_TILE)
        dist = base + (t0 - pos0).astype(F32)
        mask = (dist >= 0) & (dist < WINDOW)
        update(kw_ref[0, pl.ds(pos0, KEY_TILE), :], vwt_ref[0, :, pl.ds(pos0, KEY_TILE)], dist, mask)
    o_win = result()

    init()
    wbase = (g * nq + qi) * nwords
    r128 = lax.broadcasted_iota(jnp.int32, (KEY_TILE, Q_BLOCK), 0)
    per_word = WORD_BITS // 2

    def body(m, carry):
        word = words_ref[wbase + m // per_word]
        bits = lax.shift_right_logical(word, 2 * (m % per_word)) & 3

        @pl.when(bits != 0)
        def _():
            ra = sel_ref[0, 0, pl.ds(2 * m, 1), :]
            rb = sel_ref[0, 0, pl.ds(2 * m + 1, 1), :]
            mk = jnp.where(r128 < SLC_BLOCK, ra, rb)
            mk4 = jnp.concatenate([mk] * NSA_GROUP, axis=1)
            pos0 = pl.multiple_of(m * KEY_TILE, KEY_TILE)
            dist = base + (t0 - pos0).astype(F32)
            mask = (mk4 > 0.5) & (dist >= 0)
            update(ks_ref[0, pl.ds(pos0, KEY_TILE), :], vst_ref[0, :, pl.ds(pos0, KEY_TILE)], dist, mask)

        return carry

    lax.fori_loop(0, qi + 1, body, 0)
    o_sel = result()

    gl = gl_ref[0, 0]
    gate = 1.0 / (1.0 + jnp.exp(-gl))
    out_t = gate[0:1] * oc_ref[0, 0] + gate[1:2] * o_sel + gate[2:3] * o_win
    halves = []
    for j in range(0, NSA_GROUP, 2):
        pair = jnp.concatenate([out_t[:, j * Q_BLOCK:(j + 1) * Q_BLOCK],
                                out_t[:, (j + 1) * Q_BLOCK:(j + 2) * Q_BLOCK]], axis=0)
        halves.append(pair.T)
    o_ref[...] = jnp.concatenate(halves, axis=1).astype(o_ref.dtype)


def _nsa_sw(words, qt, glog, slopes, oct_, selt, ks, vst, kw, vwt):
    g, nq, dh, _ = qt.shape
    t = ks.shape[1]
    ns = selt.shape[2]
    nwords = words.shape[0] // (g * nq)
    grid_spec = pltpu.PrefetchScalarGridSpec(
        num_scalar_prefetch=1,
        grid=(g, nq),
        in_specs=[pl.BlockSpec((1, 1, dh, QL), lambda i, j, w: (i, j, 0, 0)),
                  pl.BlockSpec((1, 1, 3, QL), lambda i, j, w: (i, j, 0, 0)),
                  pl.BlockSpec((1, 1, QL), lambda i, j, w: (i, 0, 0)),
                  pl.BlockSpec((1, 1, dh, QL), lambda i, j, w: (i, j, 0, 0)),
                  pl.BlockSpec((1, 1, ns, Q_BLOCK), lambda i, j, w: (i, j, 0, 0)),
                  pl.BlockSpec((1, t, dh), lambda i, j, w: (i, 0, 0)),
                  pl.BlockSpec((1, dh, t), lambda i, j, w: (i, 0, 0)),
                  pl.BlockSpec((1, t, dh), lambda i, j, w: (i, 0, 0)),
                  pl.BlockSpec((1, dh, t), lambda i, j, w: (i, 0, 0))],
        out_specs=pl.BlockSpec((Q_BLOCK, NSA_GROUP * dh), lambda i, j, w: (j, i)),
        scratch_shapes=[pltpu.VMEM((1, QL), F32), pltpu.VMEM((1, QL), F32), pltpu.VMEM((dh, QL), F32)],
    )
    return pl.pallas_call(
        functools.partial(_nsa_sw_kernel, nq=nq, nwords=nwords),
        grid_spec=grid_spec,
        out_shape=jax.ShapeDtypeStruct((t, g * NSA_GROUP * dh), BF16),
        compiler_params=_params(2),
        name="nsa_sw",
    )(words, qt, glog, slopes, oct_, selt, ks, vst, kw, vwt)


def _heads_to_lanes(a, nq, width):
    a = a.reshape(nq, Q_BLOCK, NSA_KV_HEADS, NSA_GROUP, width)
    return a.transpose(2, 0, 4, 3, 1).reshape(NSA_KV_HEADS, nq, width, QL)


def _split_heads(a, t):
    return a.reshape(t, NSA_KV_HEADS, NSA_DH).transpose(1, 0, 2)


def kernel(x, a_norm_w, a_w_in, a_gnorm_w, a_w_out, a_lower_bounds, kv_norm_w, kv_w, cmp_pe_k, cmp_w1_k,
           cmp_w2_k, cmp_pe_v, cmp_w1_v, cmp_w2_v, b_norm_w, b_w_in, b_w_out, mlp_norm_w, mlp_w_up,
           mlp_w_down, final_norm_w):
    bsz, t, d = x.shape
    assert bsz == 1 and t % Q_BLOCK == 0 and t >= WINDOW + Q_BLOCK
    n_a = a_w_in.shape[0]
    n_b = b_w_in.shape[0]
    nq = t // Q_BLOCK
    xs = x[0]

    for layer in range(n_a):
        proj = _norm_matmul(xs, a_norm_w[layer], a_w_in[layer].astype(BF16))
        o = _hgrn(proj, a_lower_bounds, a_gnorm_w[layer], layer)
        xs = _proj_mlp(xs, o, a_w_out[layer].astype(BF16), mlp_norm_w[layer],
                       mlp_w_up[layer].astype(BF16), mlp_w_down[layer].astype(BF16),
                       final_norm_w, final=False)

    gd = NSA_KV_HEADS * NSA_DH
    kvw = kv_w.reshape(d, N_KV_STREAMS, gd)
    wn = jnp.concatenate([kvw[:, s] for s in (0, 1, 2, 4)], axis=1).astype(BF16)
    wt = jnp.concatenate([kvw[:, s] for s in (3, 5)], axis=1).T.astype(BF16)
    kvn, kvt = _kv_proj(xs, kv_norm_w, wn, wt)

    ncmp = t // CMP_STRIDE
    half = CMP_STRIDE * NSA_DH

    def half_blocks(a):
        return a.reshape(ncmp, CMP_STRIDE, NSA_KV_HEADS, NSA_DH).transpose(2, 0, 1, 3).reshape(
            NSA_KV_HEADS, ncmp, half)

    hf = jnp.stack([half_blocks(kvn[:, 0:gd]), half_blocks(kvn[:, gd:2 * gd])])
    pe = jnp.stack([cmp_pe_k.reshape(2, half), cmp_pe_v.reshape(2, half)])
    w1 = jnp.stack([cmp_w1_k, cmp_w1_v]).astype(BF16)
    w2 = jnp.stack([cmp_w2_k, cmp_w2_v]).astype(BF16)
    cmp_out = _compress(hf, pe, w1, w2)
    ratio = SLC_BLOCK // CMP_STRIDE
    ns = ncmp // ratio
    perm = cmp_out.reshape(2, NSA_KV_HEADS, ns, ratio, NSA_DH).transpose(0, 1, 3, 2, 4).reshape(
        2, NSA_KV_HEADS, ncmp, NSA_DH)
    kc = perm[0].astype(BF16)
    vct = perm[1].transpose(0, 2, 1).astype(BF16)
    ks = _split_heads(kvn[:, 2 * gd:3 * gd], t).astype(BF16)
    kw = _split_heads(kvn[:, 3 * gd:4 * gd], t).astype(BF16)
    vst = kvt[0:gd].reshape(NSA_KV_HEADS, NSA_DH, t)
    vwt = kvt[gd:2 * gd].reshape(NSA_KV_HEADS, NSA_DH, t)

    slopes = jnp.exp2(-8.0 * jnp.arange(1, NSA_HEADS + 1, dtype=F32) / NSA_HEADS)
    slopes = jnp.repeat(slopes.reshape(NSA_KV_HEADS, 1, NSA_GROUP), Q_BLOCK, axis=2)

    qw = NSA_HEADS * NSA_DH
    in_w = b_w_in.shape[2]
    in_pad = (-in_w) % 128
    nwords = ns // WORD_BITS
    for b in range(n_b):
        w_in = jnp.pad(b_w_in[b], ((0, 0), (0, in_pad))).astype(BF16)
        proj = _norm_matmul(xs, b_norm_w[b], w_in)
        qt = (_heads_to_lanes(proj[:, 0:qw], nq, NSA_DH) * (NSA_DH ** -0.5)).astype(BF16)
        glog = _heads_to_lanes(proj[:, qw:in_w], nq, 3)
        oct_, selt, words = _nsa_cmp(qt, kc, vct, slopes)
        words = words[:, :, 0, 0:nwords].reshape(-1)
        o = _nsa_sw(words, qt, glog, slopes, oct_, selt, ks, vst, kw, vwt)
        xs = _proj_mlp(xs, o, b_w_out[b].astype(BF16), mlp_norm_w[n_a + b],
                       mlp_w_up[n_a + b].astype(BF16), mlp_w_down[n_a + b].astype(BF16),
                       final_norm_w, final=(b == n_b - 1))
    return xs[None]
```

```python
import functools

import jax
import jax.numpy as jnp
from jax import lax
from jax.experimental import pallas as pl
from jax.experimental.pallas import tpu as pltpu

F32 = jnp.float32
BF16 = jnp.bfloat16

NORM_EPS = 1e-6
NEG_INF = -1e30
GATE_FLOOR = 1e-30

HG_HEADS = 8
HG_DK = 128
HG_DV = 128
HG_CHUNK = 64

NSA_HEADS = 16
NSA_KV_HEADS = 4
NSA_GROUP = NSA_HEADS // NSA_KV_HEADS
NSA_DH = 64
CMP_BLOCK = 32
CMP_STRIDE = 16
SLC_BLOCK = 64
SLC_TOPK = 16
WINDOW = 512
Q_BLOCK = 128
FORCE_BONUS = 1e4
N_KV_STREAMS = 6

KEY_TILE = 128
SEL_CHUNK = 4
WORD_BITS = 16
QL = NSA_GROUP * Q_BLOCK

VMEM_LIMIT = 56 * 1024 * 1024


def _params(n_axes, vmem=VMEM_LIMIT):
    return pltpu.CompilerParams(dimension_semantics=("arbitrary",) * n_axes, vmem_limit_bytes=vmem)


def _nt(a, b):
    return lax.dot_general(a, b, (((1,), (1,)), ((), ())), preferred_element_type=F32)


def _tn(a, b):
    return lax.dot_general(a, b, (((0,), (0,)), ((), ())), preferred_element_type=F32)


def _dot(a, b):
    return jnp.dot(a, b, preferred_element_type=F32)


def _rmsnorm(x, w):
    ms = jnp.mean(x * x, axis=-1, keepdims=True)
    return x * lax.rsqrt(ms + NORM_EPS) * w


def _norm_matmul_kernel(x_ref, nw_ref, w_ref, o_ref):
    h = _rmsnorm(x_ref[...], nw_ref[...]).astype(BF16)
    o_ref[...] = _dot(h, w_ref[...])


def _norm_matmul(x, nw, w, tm=256):
    t, d = x.shape
    n = w.shape[1]
    return pl.pallas_call(
        _norm_matmul_kernel,
        grid=(t // tm,),
        in_specs=[pl.BlockSpec((tm, d), lambda i: (i, 0)),
                  pl.BlockSpec((1, d), lambda i: (0, 0)),
                  pl.BlockSpec((d, n), lambda i: (0, 0))],
        out_specs=pl.BlockSpec((tm, n), lambda i: (i, 0)),
        out_shape=jax.ShapeDtypeStruct((t, n), F32),
        compiler_params=_params(1),
        name="norm_matmul",
    )(x, nw.reshape(1, d), w)


def _hgrn_kernel(proj_ref, lbp_ref, gw_ref, o_ref, st_ref, *, layer):
    c = HG_CHUNK
    hk = HG_HEADS * HG_DK

    @pl.when(pl.program_id(0) == 0)
    def _():
        st_ref[...] = jnp.zeros_like(st_ref)

    q = proj_ref[:, 0:hk]
    fp = proj_ref[:, hk:2 * hk]
    v = proj_ref[:, 2 * hk:3 * hk]
    go = proj_ref[:, 3 * hk:4 * hk]

    a = lbp_ref[...]
    e = jnp.exp(a - jnp.max(a, axis=0, keepdims=True))
    p = e / jnp.sum(e, axis=0, keepdims=True)
    cum = p[0:1]
    for i in range(1, layer + 1):
        cum = cum + p[i:i + 1]
    lb = cum - p[0:1]

    ea = jnp.exp(-jnp.abs(fp))
    r = 1.0 / (1.0 + ea)
    pos = fp >= 0
    one_m = 1.0 - lb
    f_gate = lb + one_m * jnp.where(pos, r, ea * r)
    kk = one_m * jnp.where(pos, ea * r, r)
    b = jnp.log(jnp.maximum(f_gate, GATE_FLOOR))

    row = lax.broadcasted_iota(jnp.int32, (c, hk), 0)
    for j in range(6):
        sh = 1 << j
        b = b + jnp.where(row >= sh, pltpu.roll(b, sh, 0), 0.0)
    blast = b[c - 1:c, :]

    xs = []
    z = b
    for j in range(6):
        m = 1 << j
        upper = ((row >> j) & 1) == 1
        y = jnp.where(upper, z, pltpu.roll(z, c - m, 0))
        ex = jnp.exp(jnp.where(upper, b - y, y - b))
        xs.append((jnp.where(upper, q, kk) * ex).astype(BF16))
        if j < 5:
            z = jnp.where(upper, pltpu.roll(z, m, 0), z)

    ti = lax.broadcasted_iota(jnp.int32, (c, c), 0)
    si = lax.broadcasted_iota(jnp.int32, (c, c), 1)
    masks = [(((ti >> j) & 1) == 1) & (((si >> j) & 1) == 0) & ((ti >> (j + 1)) == (si >> (j + 1)))
             for j in range(6)]
    diag = ti == si

    qb = q.astype(BF16)
    kb = kk.astype(BF16)
    vb = v.astype(BF16)
    qs = (q * jnp.exp(b)).astype(BF16)
    khat = (kk * jnp.exp(blast - b)).astype(BF16)
    eb = jnp.exp(blast)
    gw = gw_ref[...]

    for h in range(HG_HEADS):
        sl = slice(h * HG_DK, (h + 1) * HG_DK)
        attn = jnp.where(diag, _nt(qb[:, sl], kb[:, sl]), 0.0)
        for j in range(6):
            xj = xs[j][:, sl]
            attn = attn + jnp.where(masks[j], _nt(xj, xj), 0.0)
        st = st_ref[h]
        o = _dot(attn.astype(BF16), vb[:, sl]) + _nt(qs[:, sl], st.astype(BF16))
        st_ref[h] = st * eb[:, sl] + _tn(vb[:, sl], khat[:, sl])
        on = _rmsnorm(o, gw)
        g = go[:, sl]
        o_ref[:, sl] = (on * (g * (1.0 / (1.0 + jnp.exp(-g))))).astype(o_ref.dtype)


def _hgrn(proj, lb_param, gnorm_w, layer):
    t = proj.shape[0]
    hk = HG_HEADS * HG_DK
    hv = HG_HEADS * HG_DV
    n_layers = lb_param.shape[0]
    return pl.pallas_call(
        functools.partial(_hgrn_kernel, layer=layer),
        grid=(t // HG_CHUNK,),
        in_specs=[pl.BlockSpec((HG_CHUNK, 2 * hk + 2 * hv), lambda i: (i, 0)),
                  pl.BlockSpec((n_layers, hk), lambda i: (0, 0)),
                  pl.BlockSpec((1, HG_DV), lambda i: (0, 0))],
        out_specs=pl.BlockSpec((HG_CHUNK, hv), lambda i: (i, 0)),
        out_shape=jax.ShapeDtypeStruct((t, hv), BF16),
        scratch_shapes=[pltpu.VMEM((HG_HEADS, HG_DV, HG_DK), F32)],
        compiler_params=_params(1),
        name="hgrn",
    )(proj, lb_param, gnorm_w.reshape(1, HG_DV))


def _proj_mlp_kernel(x_ref, a_ref, wo_ref, nw_ref, wup_ref, wdn_ref, fnw_ref, o_ref, *, final, ff_tile):
    x1 = x_ref[...] + _dot(a_ref[...], wo_ref[...])
    h = _rmsnorm(x1, nw_ref[...]).astype(BF16)
    acc = x1
    d_ff = wup_ref.shape[1]
    for c0 in range(0, d_ff, ff_tile):
        u = jnp.maximum(_dot(h, wup_ref[:, c0:c0 + ff_tile]), 0.0)
        acc = acc + _dot((u * u).astype(BF16), wdn_ref[c0:c0 + ff_tile, :])
    if final:
        acc = _rmsnorm(acc, fnw_ref[...])
    o_ref[...] = acc


def _proj_mlp(x, a, wo, nw, wup, wdn, fnw, final, tm=256):
    t, d = x.shape
    d_ff = wup.shape[1]
    const = lambda i: (0, 0)
    return pl.pallas_call(
        functools.partial(_proj_mlp_kernel, final=final, ff_tile=min(1024, d_ff)),
        grid=(t // tm,),
        in_specs=[pl.BlockSpec((tm, d), lambda i: (i, 0)),
                  pl.BlockSpec((tm, a.shape[1]), lambda i: (i, 0)),
                  pl.BlockSpec(wo.shape, const),
                  pl.BlockSpec((1, d), const),
                  pl.BlockSpec(wup.shape, const),
                  pl.BlockSpec(wdn.shape, const),
                  pl.BlockSpec((1, d), const)],
        out_specs=pl.BlockSpec((tm, d), lambda i: (i, 0)),
        out_shape=jax.ShapeDtypeStruct((t, d), F32),
        compiler_params=_params(1),
        name="proj_mlp",
    )(x, a, wo, nw.reshape(1, d), wup, wdn, fnw.reshape(1, d))


def _kv_proj_kernel(x_ref, nw_ref, wn_ref, wt_ref, on_ref, ot_ref):
    h = _rmsnorm(x_ref[...], nw_ref[...]).astype(BF16)
    on_ref[...] = _dot(h, wn_ref[...])
    ot_ref[...] = _nt(wt_ref[...], h).astype(BF16)


def _kv_proj(x, nw, wn, wt, tm=256):
    t, d = x.shape
    const = lambda i: (0, 0)
    return pl.pallas_call(
        _kv_proj_kernel,
        grid=(t // tm,),
        in_specs=[pl.BlockSpec((tm, d), lambda i: (i, 0)),
                  pl.BlockSpec((1, d), const),
                  pl.BlockSpec(wn.shape, const),
                  pl.BlockSpec(wt.shape, const)],
        out_specs=[pl.BlockSpec((tm, wn.shape[1]), lambda i: (i, 0)),
                   pl.BlockSpec((wt.shape[0], tm), lambda i: (0, i))],
        out_shape=[jax.ShapeDtypeStruct((t, wn.shape[1]), F32),
                   jax.ShapeDtypeStruct((wt.shape[0], t), BF16)],
        compiler_params=_params(1),
        name="kv_proj",
    )(x, nw.reshape(1, d), wn, wt)


def _compress_kernel(hf_ref, pe_ref, w1_ref, w2_ref, o_ref):
    x = hf_ref[0, 0]
    nc, half = x.shape
    pe_lo = pe_ref[0, 0:1, :]
    pe_hi = pe_ref[0, 1:2, :]
    w1_lo = w1_ref[0, 0:half, :]
    w1_hi = w1_ref[0, half:2 * half, :]
    u = _dot((x + pe_lo).astype(BF16), w1_lo)
    vv = _dot((x + pe_hi).astype(BF16), w1_hi)
    nxt = pltpu.roll(vv, nc - 1, 0)
    pad = _dot(jnp.broadcast_to(pe_hi, (8, half)).astype(BF16), w1_hi)[0:1]
    row = lax.broadcasted_iota(jnp.int32, vv.shape, 0)
    pre = u + jnp.where(row == nc - 1, pad, nxt)
    hid = 0.5 * pre * (1.0 + jnp.tanh(0.7978845608028654 * (pre + 0.044715 * (pre * pre * pre))))
    o_ref[0, 0] = _dot(hid.astype(BF16), w2_ref[0])


def _compress(hf, pe, w1, w2):
    s, g, nc, half = hf.shape
    return pl.pallas_call(
        _compress_kernel,
        grid=(s, g),
        in_specs=[pl.BlockSpec((1, 1, nc, half), lambda i, j: (i, j, 0, 0)),
                  pl.BlockSpec((1, 2, half), lambda i, j: (i, 0, 0)),
                  pl.BlockSpec((1,) + w1.shape[1:], lambda i, j: (i, 0, 0)),
                  pl.BlockSpec((1,) + w2.shape[1:], lambda i, j: (i, 0, 0))],
        out_specs=pl.BlockSpec((1, 1, nc, w2.shape[2]), lambda i, j: (i, j, 0, 0)),
        out_shape=jax.ShapeDtypeStruct((s, g, nc, w2.shape[2]), F32),
        compiler_params=_params(2),
        name="compress",
    )(hf, pe, w1, w2)


def _nsa_cmp_kernel(q_ref, kc_ref, vct_ref, sl_ref, oc_ref, sel_ref, w_ref, p_ref, *, ns, topk):
    qi = pl.program_id(1)
    t0f = (qi * Q_BLOCK).astype(F32)
    qt = q_ref[0, 0]
    slope = sl_ref[0]
    ratio = SLC_BLOCK // CMP_STRIDE

    def attend(nv):
        lane = lax.broadcasted_iota(jnp.int32, (nv, QL), 1)
        rowi = lax.broadcasted_iota(jnp.int32, (nv, QL), 0)
        base = ((lane & (Q_BLOCK - 1)) - SLC_BLOCK * rowi).astype(F32)

        mx = jnp.full((1, QL), NEG_INF, F32)
        for u in range(ratio):
            s = _dot(kc_ref[0, u * ns:u * ns + nv, :], qt)
            dist = base + (t0f - float(CMP_STRIDE * u + CMP_BLOCK - 1))
            s = jnp.where(dist >= 0, s - slope * dist, NEG_INF)
            p_ref[u * ns:u * ns + nv, :] = s
            mx = jnp.maximum(mx, jnp.max(s, axis=0, keepdims=True))
        valid = mx > 0.5 * NEG_INF
        lsum = jnp.zeros((1, QL), F32)
        acc = jnp.zeros((NSA_DH, QL), F32)
        for u in range(ratio):
            p = jnp.where(valid, jnp.exp(p_ref[u * ns:u * ns + nv, :] - mx), 0.0)
            p_ref[u * ns:u * ns + nv, :] = p
            lsum = lsum + jnp.sum(p, axis=0, keepdims=True)
            acc = acc + _dot(vct_ref[0, :, u * ns:u * ns + nv], p.astype(BF16))
        inv = jnp.where(lsum > 0, 1.0 / lsum, 0.0)
        oc_ref[0, 0] = acc * inv
        imp = []
        for u in range(ratio):
            pn = p_ref[u * ns:u * ns + nv, :] * inv
            tot = pn[:, 0:Q_BLOCK]
            for j in range(1, NSA_GROUP):
                tot = tot + pn[:, j * Q_BLOCK:(j + 1) * Q_BLOCK]
            imp.append(tot)

        n_i = lax.broadcasted_iota(jnp.int32, (nv, Q_BLOCK), 0)
        ql = lax.broadcasted_iota(jnp.int32, (nv, Q_BLOCK), 1)
        pooled = jnp.where(n_i >= 1, pltpu.roll(imp[ratio - 1], 1, 0), 0.0)
        for u in range(ratio):
            pooled = pooled + imp[u]
        cur = 2 * qi + (ql >= SLC_BLOCK).astype(jnp.int32)
        ok = n_i <= cur
        forced = ok & ((n_i == 0) | (n_i == cur) | (n_i == cur - 1))
        score = jnp.where(ok & jnp.logical_not(forced), pooled, NEG_INF)
        n_f = n_i.astype(F32)
        sel = jnp.where(forced, 1.0, 0.0)
        for _ in range(topk - 3):
            best = jnp.max(score, axis=0, keepdims=True)
            idx = jnp.min(jnp.where(score == best, n_f, float(ns)), axis=0, keepdims=True)
            pick = n_f == idx
            sel = jnp.where(pick & (best > 0.5 * NEG_INF), 1.0, sel)
            score = jnp.where(pick, -jnp.inf, score)
        sel_ref[0, 0, 0:nv, :] = sel
        if nv < ns:
            sel_ref[0, 0, nv:ns, :] = jnp.zeros((ns - nv, Q_BLOCK), F32)

        any_q = jnp.max(sel, axis=1, keepdims=True)
        wk = lax.broadcasted_iota(jnp.int32, (nv, 128), 1)
        wn = lax.broadcasted_iota(jnp.int32, (nv, 128), 0)
        weight = jnp.where((wn // WORD_BITS) == wk,
                           jnp.left_shift(1, wn & (WORD_BITS - 1)).astype(F32), 0.0)
        w_ref[0, 0] = jnp.sum(any_q * weight, axis=0, keepdims=True).astype(jnp.int32)

    n_cls = 4
    step = ns // n_cls
    need = 2 * qi + 2
    for k in range(n_cls):
        lo, hi = k * step, (k + 1) * step

        @pl.when((need > lo) & (need <= hi))
        def _():
            attend(hi)


def _nsa_cmp(qt, kc, vct, slopes):
    g, nq, dh, _ = qt.shape
    ncmp = kc.shape[1]
    ns = ncmp // (SLC_BLOCK // CMP_STRIDE)
    topk = min(SLC_TOPK, ns)
    return pl.pallas_call(
        functools.partial(_nsa_cmp_kernel, ns=ns, topk=topk),
        grid=(g, nq),
        in_specs=[pl.BlockSpec((1, 1, dh, QL), lambda i, j: (i, j, 0, 0)),
                  pl.BlockSpec((1, ncmp, dh), lambda i, j: (i, 0, 0)),
                  pl.BlockSpec((1, dh, ncmp), lambda i, j: (i, 0, 0)),
                  pl.BlockSpec((1, 1, QL), lambda i, j: (i, 0, 0))],
        out_specs=[pl.BlockSpec((1, 1, dh, QL), lambda i, j: (i, j, 0, 0)),
                   pl.BlockSpec((1, 1, ns, Q_BLOCK), lambda i, j: (i, j, 0, 0)),
                   pl.BlockSpec((1, 1, 1, 128), lambda i, j: (i, j, 0, 0))],
        out_shape=[jax.ShapeDtypeStruct((g, nq, dh, QL), F32),
                   jax.ShapeDtypeStruct((g, nq, ns, Q_BLOCK), F32),
                   jax.ShapeDtypeStruct((g, nq, 1, 128), jnp.int32)],
        scratch_shapes=[pltpu.VMEM((ncmp, QL), F32)],
        compiler_params=_params(2),
        name="nsa_cmp",
    )(qt, kc, vct, slopes)


def _nsa_sw_kernel(words_ref, q_ref, gl_ref, sl_ref, oc_ref, sel_ref, base_ref, sb_ref, ks_ref, vst_ref,
                   kw_ref, vwt_ref, o_ref, kstg, vstg, rows_stg, pos_stg, m_ref, l_ref, acc_ref, *, nq, nwords):
    g = pl.program_id(0)
    qi = pl.program_id(1)
    t0 = qi * Q_BLOCK
    qt = q_ref[0, 0]
    slope = sl_ref[0]
    nkw = WINDOW + Q_BLOCK
    half = SLC_BLOCK

    start = pl.multiple_of(jnp.maximum(t0 - WINDOW, 0), KEY_TILE)
    cw = (t0 - start).astype(F32)
    basew = base_ref[...]
    s = _dot(kw_ref[0, pl.ds(start, nkw), :], qt) - sb_ref[0] - slope * cw
    s = jnp.where((basew >= -cw) & (basew < WINDOW - cw), s, NEG_INF)
    mw = jnp.max(s, axis=0, keepdims=True)
    p = jnp.exp(s - mw)
    lw = jnp.sum(p, axis=0, keepdims=True)
    o_win = _dot(vwt_ref[0, :, pl.ds(start, nkw)], p.astype(BF16)) * (1.0 / lw)

    tile_base = base_ref[0:KEY_TILE, :]
    tile_sb = sb_ref[0, 0:KEY_TILE, :]

    def sel_mask(ra, rb):
        mk = jnp.concatenate([jnp.broadcast_to(ra, (half, Q_BLOCK)), jnp.broadcast_to(rb, (half, Q_BLOCK))], axis=0)
        return jnp.concatenate([mk] * NSA_GROUP, axis=1) > 0.5

    posd = pl.multiple_of(t0, KEY_TILE)
    mask = sel_mask(sel_ref[0, 0, pl.ds(2 * qi, 1), :], sel_ref[0, 0, pl.ds(2 * qi + 1, 1), :]) & (tile_base >= 0)
    s = jnp.where(mask, _dot(ks_ref[0, pl.ds(posd, KEY_TILE), :], qt) - tile_sb, NEG_INF)
    m0 = jnp.max(s, axis=0, keepdims=True)
    p = jnp.exp(s - m0)
    m_ref[...] = m0
    l_ref[...] = jnp.sum(p, axis=0, keepdims=True)
    acc_ref[...] = _dot(vst_ref[0, :, pl.ds(posd, KEY_TILE)], p.astype(BF16))

    wbase = (g * nq + qi) * nwords
    per_word = WORD_BITS // 2

    def word_body(wi, cnt):
        word = words_ref[wbase + wi]

        def scan(cnt):
            for bp in range(per_word):
                m = wi * per_word + bp
                bits = lax.shift_right_logical(word, 2 * bp) & 3
                act = jnp.logical_and(bits != 0, m < qi)

                @pl.when(act)
                def _():
                    src = pl.multiple_of(m * KEY_TILE, KEY_TILE)
                    dst = pl.multiple_of(cnt * KEY_TILE, KEY_TILE)
                    kstg[pl.ds(dst, KEY_TILE), :] = ks_ref[0, pl.ds(src, KEY_TILE), :]
                    vstg[:, pl.ds(dst, KEY_TILE)] = vst_ref[0, :, pl.ds(src, KEY_TILE)]
                    rows_stg[cnt, 0:1, :] = sel_ref[0, 0, pl.ds(2 * m, 1), :]
                    rows_stg[cnt, 1:2, :] = sel_ref[0, 0, pl.ds(2 * m + 1, 1), :]
                    pos_stg[cnt] = m

                cnt = cnt + act.astype(jnp.int32)
            return cnt

        return lax.cond(word != 0, scan, lambda c: c, cnt)

    cnt = lax.fori_loop(0, (qi + per_word - 1) // per_word, word_body, jnp.int32(0))

    nch = (cnt + SEL_CHUNK - 1) // SEL_CHUNK
    for k in range(SEL_CHUNK - 1):
        slot = cnt + k

        @pl.when(slot < nch * SEL_CHUNK)
        def _():
            dst = pl.multiple_of(slot * KEY_TILE, KEY_TILE)
            kstg[pl.ds(dst, KEY_TILE), :] = jnp.zeros((KEY_TILE, kstg.shape[1]), kstg.dtype)
            vstg[:, pl.ds(dst, KEY_TILE)] = jnp.zeros((vstg.shape[0], KEY_TILE), vstg.dtype)
            rows_stg[slot] = jnp.zeros(rows_stg.shape[1:], F32)
            pos_stg[slot] = 0

    chunk = SEL_CHUNK * KEY_TILE

    def chunk_body(ci, carry):
        row0 = pl.multiple_of(ci * chunk, chunk)
        sc = _dot(kstg[pl.ds(row0, chunk), :], qt)
        pieces = []
        for u in range(SEL_CHUNK):
            slot = ci * SEL_CHUNK + u
            rows = rows_stg[slot]
            c = (t0 - pos_stg[slot] * KEY_TILE).astype(F32)
            su = sc[u * KEY_TILE:(u + 1) * KEY_TILE, :] - tile_sb - slope * c
            pieces.append(jnp.where(sel_mask(rows[0:1], rows[1:2]), su, NEG_INF))
        sc = jnp.concatenate(pieces, axis=0)
        m_old = m_ref[...]
        m_new = jnp.maximum(m_old, jnp.max(sc, axis=0, keepdims=True))
        alpha = jnp.exp(m_old - m_new)
        pc = jnp.exp(sc - m_new)
        l_ref[...] = alpha * l_ref[...] + jnp.sum(pc, axis=0, keepdims=True)
        acc_ref[...] = alpha * acc_ref[...] + _dot(vstg[:, pl.ds(row0, chunk)], pc.astype(BF16))
        m_ref[...] = m_new
        return carry

    lax.fori_loop(0, nch, chunk_body, 0)
    o_sel = acc_ref[...] * (1.0 / l_ref[...])

    gl = gl_ref[0, 0]
    gate = 1.0 / (1.0 + jnp.exp(-gl))
    out_t = gate[0:1] * oc_ref[0, 0] + gate[1:2] * o_sel + gate[2:3] * o_win
    halves = []
    for j in range(0, NSA_GROUP, 2):
        pair = jnp.concatenate([out_t[:, j * Q_BLOCK:(j + 1) * Q_BLOCK],
                                out_t[:, (j + 1) * Q_BLOCK:(j + 2) * Q_BLOCK]], axis=0)
        halves.append(pair.T)
    o_ref[...] = jnp.concatenate(halves, axis=1).astype(o_ref.dtype)


def _nsa_sw(words, qt, glog, slopes, oct_, selt, base, sbase, ks, vst, kw, vwt):
    g, nq, dh, _ = qt.shape
    t = ks.shape[1]
    ns = selt.shape[2]
    nkw = base.shape[0]
    nwords = words.shape[0] // (g * nq)
    grid_spec = pltpu.PrefetchScalarGridSpec(
        num_scalar_prefetch=1,
        grid=(g, nq),
        in_specs=[pl.BlockSpec((1, 1, dh, QL), lambda i, j, w: (i, j, 0, 0)),
                  pl.BlockSpec((1, 1, 3, QL), lambda i, j, w: (i, j, 0, 0)),
                  pl.BlockSpec((1, 1, QL), lambda i, j, w: (i, 0, 0)),
                  pl.BlockSpec((1, 1, dh, QL), lambda i, j, w: (i, j, 0, 0)),
                  pl.BlockSpec((1, 1, ns, Q_BLOCK), lambda i, j, w: (i, j, 0, 0)),
                  pl.BlockSpec((nkw, QL), lambda i, j, w: (0, 0)),
                  pl.BlockSpec((1, nkw, QL), lambda i, j, w: (i, 0, 0)),
                  pl.BlockSpec((1, t, dh), lambda i, j, w: (i, 0, 0)),
                  pl.BlockSpec((1, dh, t), lambda i, j, w: (i, 0, 0)),
                  pl.BlockSpec((1, t, dh), lambda i, j, w: (i, 0, 0)),
                  pl.BlockSpec((1, dh, t), lambda i, j, w: (i, 0, 0))],
        out_specs=pl.BlockSpec((Q_BLOCK, NSA_GROUP * dh), lambda i, j, w: (j, i)),
        scratch_shapes=[pltpu.VMEM((t, dh), BF16),
                        pltpu.VMEM((dh, t), BF16),
                        pltpu.VMEM((nq, 2, Q_BLOCK), F32),
                        pltpu.SMEM((nq,), jnp.int32),
                        pltpu.VMEM((1, QL), F32), pltpu.VMEM((1, QL), F32), pltpu.VMEM((dh, QL), F32)],
    )
    return pl.pallas_call(
        functools.partial(_nsa_sw_kernel, nq=nq, nwords=nwords),
        grid_spec=grid_spec,
        out_shape=jax.ShapeDtypeStruct((t, g * NSA_GROUP * dh), BF16),
        compiler_params=_params(2),
        name="nsa_sw",
    )(words, qt, glog, slopes, oct_, selt, base, sbase, ks, vst, kw, vwt)


def _heads_to_lanes(a, nq, width):
    a = a.reshape(nq, Q_BLOCK, NSA_KV_HEADS, NSA_GROUP, width)
    return a.transpose(2, 0, 4, 3, 1).reshape(NSA_KV_HEADS, nq, width, QL)


def _split_heads(a, t):
    return a.reshape(t, NSA_KV_HEADS, NSA_DH).transpose(1, 0, 2)


def kernel(x, a_norm_w, a_w_in, a_gnorm_w, a_w_out, a_lower_bounds, kv_norm_w, kv_w, cmp_pe_k, cmp_w1_k,
           cmp_w2_k, cmp_pe_v, cmp_w1_v, cmp_w2_v, b_norm_w, b_w_in, b_w_out, mlp_norm_w, mlp_w_up,
           mlp_w_down, final_norm_w):
    bsz, t, d = x.shape
    assert bsz == 1 and t % Q_BLOCK == 0 and t >= WINDOW + Q_BLOCK
    n_a = a_w_in.shape[0]
    n_b = b_w_in.shape[0]
    nq = t // Q_BLOCK
    xs = x[0]

    for layer in range(n_a):
        proj = _norm_matmul(xs, a_norm_w[layer], a_w_in[layer].astype(BF16))
        o = _hgrn(proj, a_lower_bounds, a_gnorm_w[layer], layer)
        xs = _proj_mlp(xs, o, a_w_out[layer].astype(BF16), mlp_norm_w[layer],
                       mlp_w_up[layer].astype(BF16), mlp_w_down[layer].astype(BF16),
                       final_norm_w, final=False)

    gd = NSA_KV_HEADS * NSA_DH
    kvw = kv_w.reshape(d, N_KV_STREAMS, gd)
    wn = jnp.concatenate([kvw[:, s] for s in (0, 1, 2, 4)], axis=1).astype(BF16)
    wt = jnp.concatenate([kvw[:, s] for s in (3, 5)], axis=1).T.astype(BF16)
    kvn, kvt = _kv_proj(xs, kv_norm_w, wn, wt)

    ncmp = t // CMP_STRIDE
    half = CMP_STRIDE * NSA_DH

    def half_blocks(a):
        return a.reshape(ncmp, CMP_STRIDE, NSA_KV_HEADS, NSA_DH).transpose(2, 0, 1, 3).reshape(
            NSA_KV_HEADS, ncmp, half)

    hf = jnp.stack([half_blocks(kvn[:, 0:gd]), half_blocks(kvn[:, gd:2 * gd])])
    pe = jnp.stack([cmp_pe_k.reshape(2, half), cmp_pe_v.reshape(2, half)])
    w1 = jnp.stack([cmp_w1_k, cmp_w1_v]).astype(BF16)
    w2 = jnp.stack([cmp_w2_k, cmp_w2_v]).astype(BF16)
    cmp_out = _compress(hf, pe, w1, w2)
    ratio = SLC_BLOCK // CMP_STRIDE
    ns = ncmp // ratio
    perm = cmp_out.reshape(2, NSA_KV_HEADS, ns, ratio, NSA_DH).transpose(0, 1, 3, 2, 4).reshape(
        2, NSA_KV_HEADS, ncmp, NSA_DH)
    kc = perm[0].astype(BF16)
    vct = perm[1].transpose(0, 2, 1).astype(BF16)
    ks = _split_heads(kvn[:, 2 * gd:3 * gd], t).astype(BF16)
    kw = _split_heads(kvn[:, 3 * gd:4 * gd], t).astype(BF16)
    vst = kvt[0:gd].reshape(NSA_KV_HEADS, NSA_DH, t)
    vwt = kvt[gd:2 * gd].reshape(NSA_KV_HEADS, NSA_DH, t)

    slopes = jnp.exp2(-8.0 * jnp.arange(1, NSA_HEADS + 1, dtype=F32) / NSA_HEADS)
    slopes = jnp.repeat(slopes.reshape(NSA_KV_HEADS, 1, NSA_GROUP), Q_BLOCK, axis=2)
    nkw = WINDOW + Q_BLOCK
    base = (jnp.tile(jnp.arange(Q_BLOCK, dtype=F32), NSA_GROUP)[None, :]
            - jnp.arange(nkw, dtype=F32)[:, None])
    sbase = slopes * base[None]

    qw = NSA_HEADS * NSA_DH
    in_w = b_w_in.shape[2]
    in_pad = (-in_w) % 128
    nwords = ns // WORD_BITS
    for b in range(n_b):
        w_in = jnp.pad(b_w_in[b], ((0, 0), (0, in_pad))).astype(BF16)
        proj = _norm_matmul(xs, b_norm_w[b], w_in)
        qt = (_heads_to_lanes(proj[:, 0:qw], nq, NSA_DH) * (NSA_DH ** -0.5)).astype(BF16)
        glog = _heads_to_lanes(proj[:, qw:in_w], nq, 3)
        oct_, selt, words = _nsa_cmp(qt, kc, vct, slopes)
        words = words[:, :, 0, 0:nwords].reshape(-1)
        o = _nsa_sw(words, qt, glog, slopes, oct_, selt, base, sbase, ks, vst, kw, vwt)
        xs = _proj_mlp(xs, o, b_w_out[b].astype(BF16), mlp_norm_w[n_a + b],
                       mlp_w_up[n_a + b].astype(BF16), mlp_w_down[n_a + b].astype(BF16),
                       final_norm_w, final=(b == n_b - 1))
    return xs[None]
```

```python
import functools

import jax
import jax.numpy as jnp
from jax import lax
from jax.experimental import pallas as pl
from jax.experimental.pallas import tpu as pltpu

F32 = jnp.float32
BF16 = jnp.bfloat16

NORM_EPS = 1e-6
NEG_INF = -1e30
GATE_FLOOR = 1e-30

HG_HEADS = 8
HG_DK = 128
HG_DV = 128
HG_CHUNK = 64

NSA_HEADS = 16
NSA_KV_HEADS = 4
NSA_GROUP = NSA_HEADS // NSA_KV_HEADS
NSA_DH = 64
CMP_BLOCK = 32
CMP_STRIDE = 16
SLC_BLOCK = 64
SLC_TOPK = 16
WINDOW = 512
Q_BLOCK = 128
FORCE_BONUS = 1e4
N_KV_STREAMS = 6

KEY_TILE = 128
SEL_STATIC = 8
SEL_CHUNK = 4
WORD_BITS = 16
QL = NSA_GROUP * Q_BLOCK
AUG = 128
VROWS = NSA_DH + 16

VMEM_LIMIT = 56 * 1024 * 1024


def _params(n_axes, vmem=VMEM_LIMIT):
    return pltpu.CompilerParams(dimension_semantics=("arbitrary",) * n_axes, vmem_limit_bytes=vmem)


def _nt(a, b):
    return lax.dot_general(a, b, (((1,), (1,)), ((), ())), preferred_element_type=F32)


def _tn(a, b):
    return lax.dot_general(a, b, (((0,), (0,)), ((), ())), preferred_element_type=F32)


def _dot(a, b):
    return jnp.dot(a, b, preferred_element_type=F32)


def _rmsnorm(x, w):
    ms = jnp.mean(x * x, axis=-1, keepdims=True)
    return x * lax.rsqrt(ms + NORM_EPS) * w


def _norm_matmul_kernel(x_ref, nw_ref, w_ref, o_ref):
    h = _rmsnorm(x_ref[...], nw_ref[...]).astype(BF16)
    o_ref[...] = _dot(h, w_ref[...])


def _norm_matmul(x, nw, w, tm=256):
    t, d = x.shape
    n = w.shape[1]
    return pl.pallas_call(
        _norm_matmul_kernel,
        grid=(t // tm,),
        in_specs=[pl.BlockSpec((tm, d), lambda i: (i, 0)),
                  pl.BlockSpec((1, d), lambda i: (0, 0)),
                  pl.BlockSpec((d, n), lambda i: (0, 0))],
        out_specs=pl.BlockSpec((tm, n), lambda i: (i, 0)),
        out_shape=jax.ShapeDtypeStruct((t, n), F32),
        compiler_params=_params(1),
        name="norm_matmul",
    )(x, nw.reshape(1, d), w)


def _hgrn_kernel(proj_ref, lbp_ref, gw_ref, o_ref, st_ref, *, layer):
    c = HG_CHUNK
    hk = HG_HEADS * HG_DK

    @pl.when(pl.program_id(0) == 0)
    def _():
        st_ref[...] = jnp.zeros_like(st_ref)

    q = proj_ref[:, 0:hk]
    fp = proj_ref[:, hk:2 * hk]
    v = proj_ref[:, 2 * hk:3 * hk]
    go = proj_ref[:, 3 * hk:4 * hk]

    a = lbp_ref[...]
    e = jnp.exp(a - jnp.max(a, axis=0, keepdims=True))
    p = e / jnp.sum(e, axis=0, keepdims=True)
    cum = p[0:1]
    for i in range(1, layer + 1):
        cum = cum + p[i:i + 1]
    lb = cum - p[0:1]

    ea = jnp.exp(-jnp.abs(fp))
    r = 1.0 / (1.0 + ea)
    pos = fp >= 0
    one_m = 1.0 - lb
    f_gate = lb + one_m * jnp.where(pos, r, ea * r)
    kk = one_m * jnp.where(pos, ea * r, r)
    b = jnp.log(jnp.maximum(f_gate, GATE_FLOOR))

    row = lax.broadcasted_iota(jnp.int32, (c, hk), 0)
    for j in range(6):
        sh = 1 << j
        b = b + jnp.where(row >= sh, pltpu.roll(b, sh, 0), 0.0)
    blast = b[c - 1:c, :]

    xs = []
    z = b
    for j in range(6):
        m = 1 << j
        upper = ((row >> j) & 1) == 1
        y = jnp.where(upper, z, pltpu.roll(z, c - m, 0))
        ex = jnp.exp(jnp.where(upper, b - y, y - b))
        xs.append((jnp.where(upper, q, kk) * ex).astype(BF16))
        if j < 5:
            z = jnp.where(upper, pltpu.roll(z, m, 0), z)

    ti = lax.broadcasted_iota(jnp.int32, (c, c), 0)
    si = lax.broadcasted_iota(jnp.int32, (c, c), 1)
    masks = [(((ti >> j) & 1) == 1) & (((si >> j) & 1) == 0) & ((ti >> (j + 1)) == (si >> (j + 1)))
             for j in range(6)]
    diag = ti == si

    qb = q.astype(BF16)
    kb = kk.astype(BF16)
    vb = v.astype(BF16)
    qs = (q * jnp.exp(b)).astype(BF16)
    khat = (kk * jnp.exp(blast - b)).astype(BF16)
    eb = jnp.exp(blast)
    gw = gw_ref[...]

    for h in range(HG_HEADS):
        sl = slice(h * HG_DK, (h + 1) * HG_DK)
        attn = jnp.where(diag, _nt(qb[:, sl], kb[:, sl]), 0.0)
        for j in range(6):
            xj = xs[j][:, sl]
            attn = attn + jnp.where(masks[j], _nt(xj, xj), 0.0)
        st = st_ref[h]
        o = _dot(attn.astype(BF16), vb[:, sl]) + _nt(qs[:, sl], st.astype(BF16))
        st_ref[h] = st * eb[:, sl] + _tn(vb[:, sl], khat[:, sl])
        on = _rmsnorm(o, gw)
        g = go[:, sl]
        o_ref[:, sl] = (on * (g * (1.0 / (1.0 + jnp.exp(-g))))).astype(o_ref.dtype)


def _hgrn(proj, lb_param, gnorm_w, layer):
    t = proj.shape[0]
    hk = HG_HEADS * HG_DK
    hv = HG_HEADS * HG_DV
    n_layers = lb_param.shape[0]
    return pl.pallas_call(
        functools.partial(_hgrn_kernel, layer=layer),
        grid=(t // HG_CHUNK,),
        in_specs=[pl.BlockSpec((HG_CHUNK, 2 * hk + 2 * hv), lambda i: (i, 0)),
                  pl.BlockSpec((n_layers, hk), lambda i: (0, 0)),
                  pl.BlockSpec((1, HG_DV), lambda i: (0, 0))],
        out_specs=pl.BlockSpec((HG_CHUNK, hv), lambda i: (i, 0)),
        out_shape=jax.ShapeDtypeStruct((t, hv), BF16),
        scratch_shapes=[pltpu.VMEM((HG_HEADS, HG_DV, HG_DK), F32)],
        compiler_params=_params(1),
        name="hgrn",
    )(proj, lb_param, gnorm_w.reshape(1, HG_DV))


def _proj_mlp_kernel(x_ref, a_ref, wo_ref, nw_ref, wup_ref, wdn_ref, fnw_ref, o_ref, *, final, ff_tile):
    x1 = x_ref[...] + _dot(a_ref[...], wo_ref[...])
    h = _rmsnorm(x1, nw_ref[...]).astype(BF16)
    acc = x1
    d_ff = wup_ref.shape[1]
    for c0 in range(0, d_ff, ff_tile):
        u = jnp.maximum(_dot(h, wup_ref[:, c0:c0 + ff_tile]), 0.0)
        acc = acc + _dot((u * u).astype(BF16), wdn_ref[c0:c0 + ff_tile, :])
    if final:
        acc = _rmsnorm(acc, fnw_ref[...])
    o_ref[...] = acc


def _proj_mlp(x, a, wo, nw, wup, wdn, fnw, final, tm=256):
    t, d = x.shape
    d_ff = wup.shape[1]
    const = lambda i: (0, 0)
    return pl.pallas_call(
        functools.partial(_proj_mlp_kernel, final=final, ff_tile=min(1024, d_ff)),
        grid=(t // tm,),
        in_specs=[pl.BlockSpec((tm, d), lambda i: (i, 0)),
                  pl.BlockSpec((tm, a.shape[1]), lambda i: (i, 0)),
                  pl.BlockSpec(wo.shape, const),
                  pl.BlockSpec((1, d), const),
                  pl.BlockSpec(wup.shape, const),
                  pl.BlockSpec(wdn.shape, const),
                  pl.BlockSpec((1, d), const)],
        out_specs=pl.BlockSpec((tm, d), lambda i: (i, 0)),
        out_shape=jax.ShapeDtypeStruct((t, d), F32),
        compiler_params=_params(1),
        name="proj_mlp",
    )(x, a, wo, nw.reshape(1, d), wup, wdn, fnw.reshape(1, d))


def _kv_proj_kernel(x_ref, nw_ref, wc_ref, wk_ref, wvt_ref, kc_ref, vc_ref, ks_ref, kw_ref, vst_ref, vwt_ref):
    i = pl.program_id(0)
    tm = x_ref.shape[0]
    gd = NSA_KV_HEADS * NSA_DH
    h = _rmsnorm(x_ref[...], nw_ref[...]).astype(BF16)
    cmp_in = _dot(h, wc_ref[...])
    for g in range(NSA_KV_HEADS):
        kc_ref[g] = cmp_in[:, g * NSA_DH:(g + 1) * NSA_DH]
        vc_ref[g] = cmp_in[:, gd + g * NSA_DH:gd + (g + 1) * NSA_DH]
    kk = _dot(h, wk_ref[...])
    row = lax.broadcasted_iota(jnp.int32, (tm, AUG), 0)
    lane = lax.broadcasted_iota(jnp.int32, (tm, AUG), 1)
    tok = i * tm + row
    tile = (tok // KEY_TILE).astype(F32)
    offs = (tok % KEY_TILE).astype(F32)
    consts = jnp.where((lane >= NSA_DH) & (lane < NSA_DH + 3), tile,
                       jnp.where((lane >= NSA_DH + 3) & (lane < NSA_DH + 6), 1.0,
                                 jnp.where((lane >= NSA_DH + 6) & (lane < NSA_DH + 9), offs, 0.0)))
    for g in range(NSA_KV_HEADS):
        ks_ref[g] = (kk[:, g * AUG:(g + 1) * AUG] + consts).astype(BF16)
        kw_ref[g] = (kk[:, (NSA_KV_HEADS + g) * AUG:(NSA_KV_HEADS + g + 1) * AUG] + consts).astype(BF16)
    vt = _nt(wvt_ref[...], h)
    r16 = lax.broadcasted_iota(jnp.int32, (VROWS - NSA_DH, tm), 0)
    ones_blk = jnp.where(r16 == 0, 1.0, 0.0).astype(BF16)
    for g in range(NSA_KV_HEADS):
        vst_ref[g, 0:NSA_DH, :] = vt[g * NSA_DH:(g + 1) * NSA_DH, :].astype(BF16)
        vst_ref[g, NSA_DH:VROWS, :] = ones_blk
        vwt_ref[g, 0:NSA_DH, :] = vt[gd + g * NSA_DH:gd + (g + 1) * NSA_DH, :].astype(BF16)
        vwt_ref[g, NSA_DH:VROWS, :] = ones_blk


def _kv_proj(x, nw, wc, wk, wvt, tm=256):
    t, d = x.shape
    g = NSA_KV_HEADS
    const = lambda i: (0, 0)
    rows = lambda i: (0, i, 0)
    cols = lambda i: (0, 0, i)
    return pl.pallas_call(
        _kv_proj_kernel,
        grid=(t // tm,),
        in_specs=[pl.BlockSpec((tm, d), lambda i: (i, 0)),
                  pl.BlockSpec((1, d), const),
                  pl.BlockSpec(wc.shape, const),
                  pl.BlockSpec(wk.shape, const),
                  pl.BlockSpec(wvt.shape, const)],
        out_specs=[pl.BlockSpec((g, tm, NSA_DH), rows), pl.BlockSpec((g, tm, NSA_DH), rows),
                   pl.BlockSpec((g, tm, AUG), rows), pl.BlockSpec((g, tm, AUG), rows),
                   pl.BlockSpec((g, VROWS, tm), cols), pl.BlockSpec((g, VROWS, tm), cols)],
        out_shape=[jax.ShapeDtypeStruct((g, t, NSA_DH), F32), jax.ShapeDtypeStruct((g, t, NSA_DH), F32),
                   jax.ShapeDtypeStruct((g, t, AUG), BF16), jax.ShapeDtypeStruct((g, t, AUG), BF16),
                   jax.ShapeDtypeStruct((g, VROWS, t), BF16), jax.ShapeDtypeStruct((g, VROWS, t), BF16)],
        compiler_params=_params(1),
        name="kv_proj",
    )(x, nw.reshape(1, d), wc, wk, wvt)


def _nsa_inproj_kernel(x_ref, nw_ref, wt_ref, sl_ref, q_ref, gl_ref, *, nqb):
    i = pl.program_id(0)
    qw = NSA_HEADS * NSA_DH
    h = _rmsnorm(x_ref[...], nw_ref[...]).astype(BF16)
    pt = _nt(wt_ref[...], h)
    r16 = lax.broadcasted_iota(jnp.int32, (16, QL), 0)
    ql = (lax.broadcasted_iota(jnp.int32, (1, QL), 1) & (Q_BLOCK - 1)).astype(F32)

    def split3(v):
        a = v.astype(BF16).astype(F32)
        b = (v - a).astype(BF16).astype(F32)
        return a, b, v - a - b

    for g in range(NSA_KV_HEADS):
        slope = sl_ref[g]
        s1, s2, s3 = split3(slope * float(KEY_TILE))
        r1, r2, r3 = split3(slope)
        for qb in range(nqb):
            lanes = slice(qb * Q_BLOCK, (qb + 1) * Q_BLOCK)
            for j in range(NSA_GROUP):
                r0 = (g * NSA_GROUP + j) * NSA_DH
                q_ref[g, qb, 0:NSA_DH, j * Q_BLOCK:(j + 1) * Q_BLOCK] = pt[r0:r0 + NSA_DH, lanes].astype(BF16)
            tq = ((i * nqb + qb) * Q_BLOCK).astype(F32) + ql
            o1, o2, o3 = split3(-(slope * tq))
            aug = jnp.zeros((16, QL), F32)
            for k, term in enumerate((s1, s2, s3, o1, o2, o3, r1, r2, r3)):
                aug = jnp.where(r16 == k, term, aug)
            q_ref[g, qb, NSA_DH:NSA_DH + 16, :] = aug.astype(BF16)
            q_ref[g, qb, NSA_DH + 16:AUG, :] = jnp.zeros((AUG - NSA_DH - 16, QL), BF16)
            for c in range(3):
                for j in range(NSA_GROUP):
                    r0 = qw + (g * 3 + c) * NSA_GROUP + j
                    gl_ref[g, qb, c:c + 1, j * Q_BLOCK:(j + 1) * Q_BLOCK] = pt[r0:r0 + 1, lanes]


def _nsa_inproj(x, nw, wt, slopes, nqb=2):
    t, d = x.shape
    nq = t // Q_BLOCK
    g = NSA_KV_HEADS
    tm = nqb * Q_BLOCK
    return pl.pallas_call(
        functools.partial(_nsa_inproj_kernel, nqb=nqb),
        grid=(t // tm,),
        in_specs=[pl.BlockSpec((tm, d), lambda i: (i, 0)),
                  pl.BlockSpec((1, d), lambda i: (0, 0)),
                  pl.BlockSpec(wt.shape, lambda i: (0, 0)),
                  pl.BlockSpec(slopes.shape, lambda i: (0, 0, 0))],
        out_specs=[pl.BlockSpec((g, nqb, AUG, QL), lambda i: (0, i, 0, 0)),
                   pl.BlockSpec((g, nqb, 3, QL), lambda i: (0, i, 0, 0))],
        out_shape=[jax.ShapeDtypeStruct((g, nq, AUG, QL), BF16),
                   jax.ShapeDtypeStruct((g, nq, 3, QL), F32)],
        compiler_params=_params(1),
        name="nsa_inproj",
    )(x, nw.reshape(1, d), wt, slopes)


def _compress_kernel(hf_ref, pe_ref, w1_ref, w2_ref, o_ref):
    x = hf_ref[0, 0]
    nc, half = x.shape
    pe_lo = pe_ref[0, 0:1, :]
    pe_hi = pe_ref[0, 1:2, :]
    w1_lo = w1_ref[0, 0:half, :]
    w1_hi = w1_ref[0, half:2 * half, :]
    u = _dot((x + pe_lo).astype(BF16), w1_lo)
    vv = _dot((x + pe_hi).astype(BF16), w1_hi)
    nxt = pltpu.roll(vv, nc - 1, 0)
    pad = _dot(jnp.broadcast_to(pe_hi, (8, half)).astype(BF16), w1_hi)[0:1]
    row = lax.broadcasted_iota(jnp.int32, vv.shape, 0)
    pre = u + jnp.where(row == nc - 1, pad, nxt)
    hid = 0.5 * pre * (1.0 + jnp.tanh(0.7978845608028654 * (pre + 0.044715 * (pre * pre * pre))))
    o_ref[0, 0] = _dot(hid.astype(BF16), w2_ref[0])


def _compress(hf, pe, w1, w2):
    s, g, nc, half = hf.shape
    return pl.pallas_call(
        _compress_kernel,
        grid=(s, g),
        in_specs=[pl.BlockSpec((1, 1, nc, half), lambda i, j: (i, j, 0, 0)),
                  pl.BlockSpec((1, 2, half), lambda i, j: (i, 0, 0)),
                  pl.BlockSpec((1,) + w1.shape[1:], lambda i, j: (i, 0, 0)),
                  pl.BlockSpec((1,) + w2.shape[1:], lambda i, j: (i, 0, 0))],
        out_specs=pl.BlockSpec((1, 1, nc, w2.shape[2]), lambda i, j: (i, j, 0, 0)),
        out_shape=jax.ShapeDtypeStruct((s, g, nc, w2.shape[2]), F32),
        compiler_params=_params(2),
        name="compress",
    )(hf, pe, w1, w2)


def _nsa_cmp_kernel(q_ref, kc_ref, vct_ref, sl_ref, oc_ref, sel_ref, w_ref, p_ref, *, ns, topk):
    qi = pl.program_id(1)
    t0f = (qi * Q_BLOCK).astype(F32)
    qt = q_ref[0, 0]
    slope = sl_ref[0]
    ratio = SLC_BLOCK // CMP_STRIDE

    def attend(nv):
        lane = lax.broadcasted_iota(jnp.int32, (nv, QL), 1)
        rowi = lax.broadcasted_iota(jnp.int32, (nv, QL), 0)
        base = ((lane & (Q_BLOCK - 1)) - SLC_BLOCK * rowi).astype(F32)

        mx = jnp.full((1, QL), NEG_INF, F32)
        for u in range(ratio):
            s = _dot(kc_ref[0, u * ns:u * ns + nv, :], qt)
            dist = base + (t0f - float(CMP_STRIDE * u + CMP_BLOCK - 1))
            s = jnp.where(dist >= 0, s - slope * dist, NEG_INF)
            p_ref[u * ns:u * ns + nv, :] = s
            mx = jnp.maximum(mx, jnp.max(s, axis=0, keepdims=True))
        valid = mx > 0.5 * NEG_INF
        lsum = jnp.zeros((1, QL), F32)
        acc = jnp.zeros((NSA_DH, QL), F32)
        for u in range(ratio):
            p = jnp.where(valid, jnp.exp(p_ref[u * ns:u * ns + nv, :] - mx), 0.0)
            p_ref[u * ns:u * ns + nv, :] = p
            lsum = lsum + jnp.sum(p, axis=0, keepdims=True)
            acc = acc + _dot(vct_ref[0, :, u * ns:u * ns + nv], p.astype(BF16))
        inv = jnp.where(lsum > 0, 1.0 / lsum, 0.0)
        oc_ref[0, 0] = acc * inv
        imp = []
        for u in range(ratio):
            pn = p_ref[u * ns:u * ns + nv, :] * inv
            tot = pn[:, 0:Q_BLOCK]
            for j in range(1, NSA_GROUP):
                tot = tot + pn[:, j * Q_BLOCK:(j + 1) * Q_BLOCK]
            imp.append(tot)

        n_i = lax.broadcasted_iota(jnp.int32, (nv, Q_BLOCK), 0)
        ql = lax.broadcasted_iota(jnp.int32, (nv, Q_BLOCK), 1)
        pooled = jnp.where(n_i >= 1, pltpu.roll(imp[ratio - 1], 1, 0), 0.0)
        for u in range(ratio):
            pooled = pooled + imp[u]
        cur = 2 * qi + (ql >= SLC_BLOCK).astype(jnp.int32)
        ok = n_i <= cur
        forced = ok & ((n_i == 0) | (n_i == cur) | (n_i == cur - 1))
        score = jnp.where(ok & jnp.logical_not(forced), pooled, NEG_INF)
        n_f = n_i.astype(F32)
        sel = jnp.where(forced, 1.0, 0.0)
        for _ in range(topk - 3):
            best = jnp.max(score, axis=0, keepdims=True)
            idx = jnp.min(jnp.where(score == best, n_f, float(ns)), axis=0, keepdims=True)
            pick = n_f == idx
            sel = jnp.where(pick & (best > 0.5 * NEG_INF), 1.0, sel)
            score = jnp.where(pick, -jnp.inf, score)
        sel_ref[0, 0, 0:nv, :] = sel
        if nv < ns:
            sel_ref[0, 0, nv:ns, :] = jnp.zeros((ns - nv, Q_BLOCK), F32)

        any_q = jnp.max(sel, axis=1, keepdims=True)
        wk = lax.broadcasted_iota(jnp.int32, (nv, 128), 1)
        wn = lax.broadcasted_iota(jnp.int32, (nv, 128), 0)
        weight = jnp.where((wn // WORD_BITS) == wk,
                           jnp.left_shift(1, wn & (WORD_BITS - 1)).astype(F32), 0.0)
        w_ref[0, 0] = jnp.sum(any_q * weight, axis=0, keepdims=True).astype(jnp.int32)

    n_cls = 4
    step = ns // n_cls
    need = 2 * qi + 2
    for k in range(n_cls):
        lo, hi = k * step, (k + 1) * step

        @pl.when((need > lo) & (need <= hi))
        def _():
            attend(hi)


def _nsa_cmp(qt, kc, vct, slopes):
    g, nq, _, _ = qt.shape
    dh = NSA_DH
    ncmp = kc.shape[1]
    ns = ncmp // (SLC_BLOCK // CMP_STRIDE)
    topk = min(SLC_TOPK, ns)
    return pl.pallas_call(
        functools.partial(_nsa_cmp_kernel, ns=ns, topk=topk),
        grid=(g, nq),
        in_specs=[pl.BlockSpec((1, 1, dh, QL), lambda i, j: (i, j, 0, 0)),
                  pl.BlockSpec((1, ncmp, dh), lambda i, j: (i, 0, 0)),
                  pl.BlockSpec((1, dh, ncmp), lambda i, j: (i, 0, 0)),
                  pl.BlockSpec((1, 1, QL), lambda i, j: (i, 0, 0))],
        out_specs=[pl.BlockSpec((1, 1, dh, QL), lambda i, j: (i, j, 0, 0)),
                   pl.BlockSpec((1, 1, ns, Q_BLOCK), lambda i, j: (i, j, 0, 0)),
                   pl.BlockSpec((1, 1, 1, 128), lambda i, j: (i, j, 0, 0))],
        out_shape=[jax.ShapeDtypeStruct((g, nq, dh, QL), F32),
                   jax.ShapeDtypeStruct((g, nq, ns, Q_BLOCK), F32),
                   jax.ShapeDtypeStruct((g, nq, 1, 128), jnp.int32)],
        scratch_shapes=[pltpu.VMEM((ncmp, QL), F32)],
        compiler_params=_params(2),
        name="nsa_cmp",
    )(qt, kc, vct, slopes)


def _nsa_sw_kernel(words_ref, q_ref, gl_ref, oc_ref, sel_ref, base_ref, ks_ref, vst_ref,
                   kw_ref, vwt_ref, o_ref, list_ref, m_ref, acc_ref, *, nq, nwords):
    g = pl.program_id(0)
    qi = pl.program_id(1)
    t0 = qi * Q_BLOCK
    qa = q_ref[0, 0]
    half = SLC_BLOCK
    tile_base = base_ref[...]

    for u in range(SEL_STATIC):
        list_ref[u] = 0
    wbase = (g * nq + qi) * nwords
    per_word = WORD_BITS // 2

    def word_body(wi, cnt):
        word = words_ref[wbase + wi]

        def scan(cnt):
            for bp in range(per_word):
                m = wi * per_word + bp
                bits = lax.shift_right_logical(word, 2 * bp) & 3
                list_ref[cnt] = m
                cnt = cnt + jnp.logical_and(bits != 0, m < qi).astype(jnp.int32)
            return cnt

        return lax.cond(word != 0, scan, lambda c: c, cnt)

    cnt = lax.fori_loop(0, (qi + per_word - 1) // per_word, word_body, jnp.int32(0))
    for k in range(1, SEL_CHUNK):
        list_ref[jnp.minimum(cnt + k, nq - 1)] = 0

    def probs(s, mx):
        return jnp.exp((s - mx).astype(BF16))

    def sel_mask(ra, rb):
        mk = jnp.concatenate([jnp.broadcast_to(ra, (half, Q_BLOCK)), jnp.broadcast_to(rb, (half, Q_BLOCK))], axis=0)
        return jnp.concatenate([mk] * NSA_GROUP, axis=1) > 0.5

    def listed_tile(slot):
        live = slot < cnt
        m = jnp.where(live, list_ref[jnp.minimum(slot, nq - 1)], 0)
        keep = jnp.where(live, 1.0, 0.0)
        pos = pl.multiple_of(m * KEY_TILE, KEY_TILE)
        mask = sel_mask(sel_ref[0, 0, pl.ds(2 * m, 1), :] * keep, sel_ref[0, 0, pl.ds(2 * m + 1, 1), :] * keep)
        return ks_ref[0, pl.ds(pos, KEY_TILE), :], vst_ref[0, :, pl.ds(pos, KEY_TILE)], mask

    def masked_scores(k_tiles, masks):
        sc = _dot(jnp.concatenate(k_tiles, axis=0), qa)
        tiles, mx = [], None
        for u, mask in enumerate(masks):
            s = jnp.where(mask, sc[u * KEY_TILE:(u + 1) * KEY_TILE, :], NEG_INF)
            mu = jnp.max(s, axis=0, keepdims=True)
            mx = mu if mx is None else jnp.maximum(mx, mu)
            tiles.append(s)
        return tiles, mx

    def pv(s_tiles, v_tiles, mx):
        acc = None
        for u in range(0, len(s_tiles), 2):
            d = _dot(jnp.concatenate(v_tiles[u:u + 2], axis=1),
                     probs(jnp.concatenate(s_tiles[u:u + 2], axis=0), mx))
            acc = d if acc is None else acc + d
        return acc

    n_wt = (WINDOW + Q_BLOCK) // KEY_TILE
    wk, wv, wmask = [], [], []
    for u in range(n_wt):
        pos_raw = t0 - WINDOW + u * KEY_TILE
        pos = pl.multiple_of(jnp.maximum(pos_raw, 0), KEY_TILE)
        wk.append(kw_ref[0, pl.ds(pos, KEY_TILE), :])
        wv.append(vwt_ref[0, :, pl.ds(pos, KEY_TILE)])
        live = pos_raw >= 0
        if u == 0:
            wmask.append(jnp.logical_and(tile_base < 0, live))
        elif u == n_wt - 1:
            wmask.append(tile_base >= 0)
        else:
            wmask.append(live)
    wtiles, mw = masked_scores(wk, wmask)
    accw = pv(wtiles, wv, mw)
    o_win = accw[0:NSA_DH] * (1.0 / accw[NSA_DH:NSA_DH + 1])

    posd = pl.multiple_of(t0, KEY_TILE)
    diag_mask = sel_mask(sel_ref[0, 0, pl.ds(2 * qi, 1), :], sel_ref[0, 0, pl.ds(2 * qi + 1, 1), :]) & (tile_base >= 0)
    listed = [listed_tile(jnp.int32(u)) for u in range(SEL_STATIC)]
    stiles, m0 = masked_scores([ks_ref[0, pl.ds(posd, KEY_TILE), :]] + [t[0] for t in listed],
                               [diag_mask] + [t[2] for t in listed])
    m_ref[...] = m0
    acc_ref[...] = pv(stiles, [vst_ref[0, :, pl.ds(posd, KEY_TILE)]] + [t[1] for t in listed], m0)

    def chunk_body(ci, carry):
        listed = [listed_tile(SEL_STATIC + ci * SEL_CHUNK + u) for u in range(SEL_CHUNK)]
        tiles, cmax = masked_scores([t[0] for t in listed], [t[2] for t in listed])
        m_old = m_ref[...]
        m_new = jnp.maximum(m_old, cmax)
        acc_ref[...] = jnp.exp(m_old - m_new) * acc_ref[...] + pv(tiles, [t[1] for t in listed], m_new)
        m_ref[...] = m_new
        return carry

    lax.fori_loop(0, (jnp.maximum(cnt - SEL_STATIC, 0) + SEL_CHUNK - 1) // SEL_CHUNK, chunk_body, 0)
    o_sel = acc_ref[0:NSA_DH, :] * (1.0 / acc_ref[NSA_DH:NSA_DH + 1, :])

    gl = gl_ref[0, 0]
    gate = 1.0 / (1.0 + jnp.exp(-gl))
    out_t = gate[0:1] * oc_ref[0, 0] + gate[1:2] * o_sel + gate[2:3] * o_win
    halves = []
    for j in range(0, NSA_GROUP, 2):
        pair = jnp.concatenate([out_t[:, j * Q_BLOCK:(j + 1) * Q_BLOCK],
                                out_t[:, (j + 1) * Q_BLOCK:(j + 2) * Q_BLOCK]], axis=0)
        halves.append(pair.T)
    o_ref[...] = jnp.concatenate(halves, axis=1).astype(o_ref.dtype)


def _nsa_sw(words, qa, glog, oct_, selt, base, ks, vst, kw, vwt):
    g, nq, _, _ = qa.shape
    dh = NSA_DH
    t = ks.shape[1]
    ns = selt.shape[2]
    nwords = words.shape[0] // (g * nq)
    grid_spec = pltpu.PrefetchScalarGridSpec(
        num_scalar_prefetch=1,
        grid=(g, nq),
        in_specs=[pl.BlockSpec((1, 1, AUG, QL), lambda i, j, w: (i, j, 0, 0)),
                  pl.BlockSpec((1, 1, 3, QL), lambda i, j, w: (i, j, 0, 0)),
                  pl.BlockSpec((1, 1, dh, QL), lambda i, j, w: (i, j, 0, 0)),
                  pl.BlockSpec((1, 1, ns, Q_BLOCK), lambda i, j, w: (i, j, 0, 0)),
                  pl.BlockSpec((KEY_TILE, QL), lambda i, j, w: (0, 0)),
                  pl.BlockSpec((1, t, AUG), lambda i, j, w: (i, 0, 0)),
                  pl.BlockSpec((1, VROWS, t), lambda i, j, w: (i, 0, 0)),
                  pl.BlockSpec((1, t, AUG), lambda i, j, w: (i, 0, 0)),
                  pl.BlockSpec((1, VROWS, t), lambda i, j, w: (i, 0, 0))],
        out_specs=pl.BlockSpec((Q_BLOCK, NSA_GROUP * dh), lambda i, j, w: (j, i)),
        scratch_shapes=[pltpu.SMEM((nq,), jnp.int32),
                        pltpu.VMEM((1, QL), F32), pltpu.VMEM((VROWS, QL), F32)],
    )
    return pl.pallas_call(
        functools.partial(_nsa_sw_kernel, nq=nq, nwords=nwords),
        grid_spec=grid_spec,
        out_shape=jax.ShapeDtypeStruct((t, g * NSA_GROUP * dh), BF16),
        compiler_params=_params(2),
        name="nsa_sw",
    )(words, qa, glog, oct_, selt, base, ks, vst, kw, vwt)


def kernel(x, a_norm_w, a_w_in, a_gnorm_w, a_w_out, a_lower_bounds, kv_norm_w, kv_w, cmp_pe_k, cmp_w1_k,
           cmp_w2_k, cmp_pe_v, cmp_w1_v, cmp_w2_v, b_norm_w, b_w_in, b_w_out, mlp_norm_w, mlp_w_up,
           mlp_w_down, final_norm_w):
    bsz, t, d = x.shape
    assert bsz == 1 and t % Q_BLOCK == 0 and t >= WINDOW + Q_BLOCK
    n_a = a_w_in.shape[0]
    n_b = b_w_in.shape[0]
    nq = t // Q_BLOCK
    xs = x[0]

    for layer in range(n_a):
        proj = _norm_matmul(xs, a_norm_w[layer], a_w_in[layer].astype(BF16))
        o = _hgrn(proj, a_lower_bounds, a_gnorm_w[layer], layer)
        xs = _proj_mlp(xs, o, a_w_out[layer].astype(BF16), mlp_norm_w[layer],
                       mlp_w_up[layer].astype(BF16), mlp_w_down[layer].astype(BF16),
                       final_norm_w, final=False)

    gd = NSA_KV_HEADS * NSA_DH
    kvw = kv_w.reshape(d, N_KV_STREAMS, NSA_KV_HEADS, NSA_DH)
    wc = kvw[:, 0:2].reshape(d, 2 * gd).astype(BF16)
    wk = jnp.pad(jnp.stack([kvw[:, 2], kvw[:, 4]], axis=1), ((0, 0), (0, 0), (0, 0), (0, AUG - NSA_DH)))
    wk = wk.reshape(d, 2 * NSA_KV_HEADS * AUG).astype(BF16)
    wvt = jnp.stack([kvw[:, 3], kvw[:, 5]], axis=1).reshape(d, 2 * gd).T.astype(BF16)
    kc_in, vc_in, ks, kw, vst, vwt = _kv_proj(xs, kv_norm_w, wc, wk, wvt)

    ncmp = t // CMP_STRIDE
    half = CMP_STRIDE * NSA_DH
    hf = jnp.stack([kc_in.reshape(NSA_KV_HEADS, ncmp, half), vc_in.reshape(NSA_KV_HEADS, ncmp, half)])
    pe = jnp.stack([cmp_pe_k.reshape(2, half), cmp_pe_v.reshape(2, half)])
    w1 = jnp.stack([cmp_w1_k, cmp_w1_v]).astype(BF16)
    w2 = jnp.stack([cmp_w2_k, cmp_w2_v]).astype(BF16)
    cmp_out = _compress(hf, pe, w1, w2)
    ratio = SLC_BLOCK // CMP_STRIDE
    ns = ncmp // ratio
    perm = cmp_out.reshape(2, NSA_KV_HEADS, ns, ratio, NSA_DH).transpose(0, 1, 3, 2, 4).reshape(
        2, NSA_KV_HEADS, ncmp, NSA_DH)
    kc = perm[0].astype(BF16)
    vct = perm[1].transpose(0, 2, 1).astype(BF16)

    slopes = jnp.exp2(-8.0 * jnp.arange(1, NSA_HEADS + 1, dtype=F32) / NSA_HEADS)
    slopes = jnp.repeat(slopes.reshape(NSA_KV_HEADS, 1, NSA_GROUP), Q_BLOCK, axis=2)
    base = (jnp.tile(jnp.arange(Q_BLOCK, dtype=F32), NSA_GROUP)[None, :]
            - jnp.arange(KEY_TILE, dtype=F32)[:, None])

    qw = NSA_HEADS * NSA_DH
    n_gate = b_w_in.shape[2] - qw
    pad_rows = (-(qw + n_gate)) % 128
    nwords = ns // WORD_BITS
    for b in range(n_b):
        w_q = b_w_in[b][:, 0:qw] * (NSA_DH ** -0.5)
        w_g = b_w_in[b][:, qw:].reshape(d, NSA_KV_HEADS, NSA_GROUP, 3).transpose(0, 1, 3, 2).reshape(d, n_gate)
        wt_in = jnp.pad(jnp.concatenate([w_q, w_g], axis=1), ((0, 0), (0, pad_rows))).T.astype(BF16)
        qa, glog = _nsa_inproj(xs, b_norm_w[b], wt_in, slopes)
        oct_, selt, words = _nsa_cmp(qa, kc, vct, slopes)
        words = words[:, :, 0, 0:nwords].reshape(-1)
        o = _nsa_sw(words, qa, glog, oct_, selt, base, ks, vst, kw, vwt)
        xs = _proj_mlp(xs, o, b_w_out[b].astype(BF16), mlp_norm_w[n_a + b],
                       mlp_w_up[n_a + b].astype(BF16), mlp_w_down[n_a + b].astype(BF16),
                       final_norm_w, final=(b == n_b - 1))
    return xs[None]
```

```python
import functools

import jax
import jax.numpy as jnp
from jax import lax
from jax.experimental import pallas as pl
from jax.experimental.pallas import tpu as pltpu

F32 = jnp.float32
BF16 = jnp.bfloat16

NORM_EPS = 1e-6
NEG_INF = -1e30
GATE_FLOOR = 1e-30

HG_HEADS = 8
HG_DK = 128
HG_DV = 128
HG_CHUNK = 64
HG_STEP_CHUNKS = 4

NSA_HEADS = 16
NSA_KV_HEADS = 4
NSA_GROUP = NSA_HEADS // NSA_KV_HEADS
NSA_DH = 64
CMP_BLOCK = 32
CMP_STRIDE = 16
SLC_BLOCK = 64
SLC_TOPK = 16
WINDOW = 512
Q_BLOCK = 128
FORCE_BONUS = 1e4
N_KV_STREAMS = 6

KEY_TILE = 128
SEL_STATIC = 8
SEL_CHUNK = 4
WORD_BITS = 16
QL = NSA_GROUP * Q_BLOCK
AUG = 128
VROWS = NSA_DH + 16

VMEM_LIMIT = 56 * 1024 * 1024


def _params(n_axes, vmem=VMEM_LIMIT):
    return pltpu.CompilerParams(dimension_semantics=("arbitrary",) * n_axes, vmem_limit_bytes=vmem)


def _nt(a, b):
    return lax.dot_general(a, b, (((1,), (1,)), ((), ())), preferred_element_type=F32)


def _tn(a, b):
    return lax.dot_general(a, b, (((0,), (0,)), ((), ())), preferred_element_type=F32)


def _dot(a, b):
    return jnp.dot(a, b, preferred_element_type=F32)


def _rmsnorm(x, w):
    ms = jnp.mean(x * x, axis=-1, keepdims=True)
    return x * lax.rsqrt(ms + NORM_EPS) * w


def _norm_matmul_kernel(x_ref, nw_ref, w_ref, o_ref):
    h = _rmsnorm(x_ref[...], nw_ref[...]).astype(BF16)
    o_ref[...] = _dot(h, w_ref[...])


def _norm_matmul(x, nw, w, tm=256):
    t, d = x.shape
    n = w.shape[1]
    return pl.pallas_call(
        _norm_matmul_kernel,
        grid=(t // tm,),
        in_specs=[pl.BlockSpec((tm, d), lambda i: (i, 0)),
                  pl.BlockSpec((1, d), lambda i: (0, 0)),
                  pl.BlockSpec((d, n), lambda i: (0, 0))],
        out_specs=pl.BlockSpec((tm, n), lambda i: (i, 0)),
        out_shape=jax.ShapeDtypeStruct((t, n), F32),
        compiler_params=_params(1),
        name="norm_matmul",
    )(x, nw.reshape(1, d), w)


def _hgrn_kernel(proj_ref, lbp_ref, gw_ref, seg_ref, o_ref, st_ref, *, layer, n_chunks):
    c = HG_CHUNK
    hk = HG_HEADS * HG_DK

    @pl.when(pl.program_id(0) == 0)
    def _():
        st_ref[...] = jnp.zeros_like(st_ref)

    a = lbp_ref[...]
    e = jnp.exp(a - jnp.max(a, axis=0, keepdims=True))
    p = e / jnp.sum(e, axis=0, keepdims=True)
    cum = p[0:1]
    for i in range(1, layer + 1):
        cum = cum + p[i:i + 1]
    lb = cum - p[0:1]
    one_m = 1.0 - lb

    sub = lax.broadcasted_iota(jnp.int32, (8, hk), 0)
    ti = lax.broadcasted_iota(jnp.int32, (c, c), 0)
    si = lax.broadcasted_iota(jnp.int32, (c, c), 1)
    masks = [(((ti >> j) & 1) == 1) & (((si >> j) & 1) == 0) & ((ti >> (j + 1)) == (si >> (j + 1)))
             for j in range(6)]
    diag = ti == si
    gw = gw_ref[...]

    for ch in range(n_chunks):
        rows = slice(ch * c, (ch + 1) * c)
        q = proj_ref[rows, 0:hk]
        fp = proj_ref[rows, hk:2 * hk]
        v = proj_ref[rows, 2 * hk:3 * hk]
        go = proj_ref[rows, 3 * hk:4 * hk]

        ea = jnp.exp(-jnp.abs(fp))
        r = 1.0 / (1.0 + ea)
        pos = fp >= 0
        f_gate = lb + one_m * jnp.where(pos, r, ea * r)
        kk = one_m * jnp.where(pos, ea * r, r)
        logf = jnp.log(jnp.maximum(f_gate, GATE_FLOOR))

        hi = logf.astype(BF16)
        rem = logf - hi.astype(F32)
        mid = rem.astype(BF16)
        lo = (rem - mid.astype(F32)).astype(BF16)
        sums = _dot(seg_ref[...], jnp.concatenate([hi, mid, lo], axis=0))
        b = sums[0:c]
        eb = jnp.exp(b[c - 1:c, :])

        xs = []
        for j in range(6):
            ex = jnp.exp(sums[c * (j + 1):c * (j + 2)])
            slabs = []
            for r0 in range(0, c, 8):
                if j < 3:
                    slabs.append(jnp.where(((sub >> j) & 1) == 1, q[r0:r0 + 8], kk[r0:r0 + 8]))
                else:
                    slabs.append(q[r0:r0 + 8] if (r0 >> j) & 1 else kk[r0:r0 + 8])
            xs.append((jnp.concatenate(slabs, axis=0) * ex).astype(BF16))

        qb = q.astype(BF16)
        kb = kk.astype(BF16)
        vb = v.astype(BF16)
        qs = (q * jnp.exp(b)).astype(BF16)
        khat = (kk * jnp.exp(sums[7 * c:8 * c])).astype(BF16)

        heads = [slice(h * HG_DK, (h + 1) * HG_DK) for h in range(HG_HEADS)]
        attn = [jnp.where(diag, _nt(qb[:, sl], kb[:, sl]), 0.0) for sl in heads]
        for j in range(6):
            for h, sl in enumerate(heads):
                xj = xs[j][:, sl]
                attn[h] = attn[h] + jnp.where(masks[j], _nt(xj, xj), 0.0)
        outs = []
        for h, sl in enumerate(heads):
            st = st_ref[h]
            outs.append(_dot(attn[h].astype(BF16), vb[:, sl]) + _nt(qs[:, sl], st.astype(BF16)))
            st_ref[h] = st * eb[:, sl] + _tn(vb[:, sl], khat[:, sl])
        for h, sl in enumerate(heads):
            g = go[:, sl]
            o_ref[rows, sl] = (_rmsnorm(outs[h], gw) * (g * (1.0 / (1.0 + jnp.exp(-g))))).astype(o_ref.dtype)


def _segment_matrix():
    c = HG_CHUNK
    t = jnp.arange(c)[:, None]
    i = jnp.arange(c)[None, :]
    parts = [i <= t]
    for j in range(6):
        m = 1 << j
        ref = t - (t % (2 * m)) + m
        parts.append(jnp.where(t >= ref, (i > ref) & (i <= t), (i > t) & (i <= ref)))
    parts.append(i > t)
    return jnp.tile(jnp.concatenate(parts, axis=0), (1, 3)).astype(BF16)


def _hgrn(proj, lb_param, gnorm_w, layer):
    t = proj.shape[0]
    hk = HG_HEADS * HG_DK
    hv = HG_HEADS * HG_DV
    n_layers = lb_param.shape[0]
    seg = _segment_matrix()
    tm = HG_STEP_CHUNKS * HG_CHUNK
    return pl.pallas_call(
        functools.partial(_hgrn_kernel, layer=layer, n_chunks=HG_STEP_CHUNKS),
        grid=(t // tm,),
        in_specs=[pl.BlockSpec((tm, 2 * hk + 2 * hv), lambda i: (i, 0)),
                  pl.BlockSpec((n_layers, hk), lambda i: (0, 0)),
                  pl.BlockSpec((1, HG_DV), lambda i: (0, 0)),
                  pl.BlockSpec(seg.shape, lambda i: (0, 0))],
        out_specs=pl.BlockSpec((tm, hv), lambda i: (i, 0)),
        out_shape=jax.ShapeDtypeStruct((t, hv), BF16),
        scratch_shapes=[pltpu.VMEM((HG_HEADS, HG_DV, HG_DK), F32)],
        compiler_params=_params(1),
        name="hgrn",
    )(proj, lb_param, gnorm_w.reshape(1, HG_DV), seg)


def _proj_mlp_kernel(x_ref, a_ref, wo_ref, nw_ref, wup_ref, wdn_ref, fnw_ref, o_ref, *, final, ff_tile):
    x1 = x_ref[...] + _dot(a_ref[...], wo_ref[...])
    h = _rmsnorm(x1, nw_ref[...]).astype(BF16)
    acc = x1
    d_ff = wup_ref.shape[1]
    for c0 in range(0, d_ff, ff_tile):
        u = jnp.maximum(_dot(h, wup_ref[:, c0:c0 + ff_tile]), 0.0)
        acc = acc + _dot((u * u).astype(BF16), wdn_ref[c0:c0 + ff_tile, :])
    if final:
        acc = _rmsnorm(acc, fnw_ref[...])
    o_ref[...] = acc


def _proj_mlp(x, a, wo, nw, wup, wdn, fnw, final, tm=256):
    t, d = x.shape
    d_ff = wup.shape[1]
    const = lambda i: (0, 0)
    return pl.pallas_call(
        functools.partial(_proj_mlp_kernel, final=final, ff_tile=min(1024, d_ff)),
        grid=(t // tm,),
        in_specs=[pl.BlockSpec((tm, d), lambda i: (i, 0)),
                  pl.BlockSpec((tm, a.shape[1]), lambda i: (i, 0)),
                  pl.BlockSpec(wo.shape, const),
                  pl.BlockSpec((1, d), const),
                  pl.BlockSpec(wup.shape, const),
                  pl.BlockSpec(wdn.shape, const),
                  pl.BlockSpec((1, d), const)],
        out_specs=pl.BlockSpec((tm, d), lambda i: (i, 0)),
        out_shape=jax.ShapeDtypeStruct((t, d), F32),
        compiler_params=_params(1),
        name="proj_mlp",
    )(x, a, wo, nw.reshape(1, d), wup, wdn, fnw.reshape(1, d))


def _kv_proj_kernel(x_ref, nw_ref, wc_ref, wk_ref, wvt_ref, kc_ref, vc_ref, ks_ref, kw_ref, vst_ref, vwt_ref):
    i = pl.program_id(0)
    tm = x_ref.shape[0]
    gd = NSA_KV_HEADS * NSA_DH
    h = _rmsnorm(x_ref[...], nw_ref[...]).astype(BF16)
    cmp_in = _dot(h, wc_ref[...])
    for g in range(NSA_KV_HEADS):
        kc_ref[g] = cmp_in[:, g * NSA_DH:(g + 1) * NSA_DH]
        vc_ref[g] = cmp_in[:, gd + g * NSA_DH:gd + (g + 1) * NSA_DH]
    kk = _dot(h, wk_ref[...])
    row = lax.broadcasted_iota(jnp.int32, (tm, AUG), 0)
    lane = lax.broadcasted_iota(jnp.int32, (tm, AUG), 1)
    tok = i * tm + row
    tile = (tok // KEY_TILE).astype(F32)
    offs = (tok % KEY_TILE).astype(F32)
    consts = jnp.where((lane >= NSA_DH) & (lane < NSA_DH + 3), tile,
                       jnp.where((lane >= NSA_DH + 3) & (lane < NSA_DH + 6), 1.0,
                                 jnp.where((lane >= NSA_DH + 6) & (lane < NSA_DH + 9), offs, 0.0)))
    for g in range(NSA_KV_HEADS):
        ks_ref[g] = (kk[:, g * AUG:(g + 1) * AUG] + consts).astype(BF16)
        kw_ref[g] = (kk[:, (NSA_KV_HEADS + g) * AUG:(NSA_KV_HEADS + g + 1) * AUG] + consts).astype(BF16)
    vt = _nt(wvt_ref[...], h)
    r16 = lax.broadcasted_iota(jnp.int32, (VROWS - NSA_DH, tm), 0)
    ones_blk = jnp.where(r16 == 0, 1.0, 0.0).astype(BF16)
    for g in range(NSA_KV_HEADS):
        vst_ref[g, 0:NSA_DH, :] = vt[g * NSA_DH:(g + 1) * NSA_DH, :].astype(BF16)
        vst_ref[g, NSA_DH:VROWS, :] = ones_blk
        vwt_ref[g, 0:NSA_DH, :] = vt[gd + g * NSA_DH:gd + (g + 1) * NSA_DH, :].astype(BF16)
        vwt_ref[g, NSA_DH:VROWS, :] = ones_blk


def _kv_proj(x, nw, wc, wk, wvt, tm=256):
    t, d = x.shape
    g = NSA_KV_HEADS
    const = lambda i: (0, 0)
    rows = lambda i: (0, i, 0)
    cols = lambda i: (0, 0, i)
    return pl.pallas_call(
        _kv_proj_kernel,
        grid=(t // tm,),
        in_specs=[pl.BlockSpec((tm, d), lambda i: (i, 0)),
                  pl.BlockSpec((1, d), const),
                  pl.BlockSpec(wc.shape, const),
                  pl.BlockSpec(wk.shape, const),
                  pl.BlockSpec(wvt.shape, const)],
        out_specs=[pl.BlockSpec((g, tm, NSA_DH), rows), pl.BlockSpec((g, tm, NSA_DH), rows),
                   pl.BlockSpec((g, tm, AUG), rows), pl.BlockSpec((g, tm, AUG), rows),
                   pl.BlockSpec((g, VROWS, tm), cols), pl.BlockSpec((g, VROWS, tm), cols)],
        out_shape=[jax.ShapeDtypeStruct((g, t, NSA_DH), F32), jax.ShapeDtypeStruct((g, t, NSA_DH), F32),
                   jax.ShapeDtypeStruct((g, t, AUG), BF16), jax.ShapeDtypeStruct((g, t, AUG), BF16),
                   jax.ShapeDtypeStruct((g, VROWS, t), BF16), jax.ShapeDtypeStruct((g, VROWS, t), BF16)],
        compiler_params=_params(1),
        name="kv_proj",
    )(x, nw.reshape(1, d), wc, wk, wvt)


def _nsa_inproj_kernel(x_ref, nw_ref, wt_ref, sl_ref, q_ref, gl_ref, *, nqb):
    i = pl.program_id(0)
    qw = NSA_HEADS * NSA_DH
    h = _rmsnorm(x_ref[...], nw_ref[...]).astype(BF16)
    pt = _nt(wt_ref[...], h)
    r16 = lax.broadcasted_iota(jnp.int32, (16, QL), 0)
    ql = (lax.broadcasted_iota(jnp.int32, (1, QL), 1) & (Q_BLOCK - 1)).astype(F32)

    def split3(v):
        a = v.astype(BF16).astype(F32)
        b = (v - a).astype(BF16).astype(F32)
        return a, b, v - a - b

    for g in range(NSA_KV_HEADS):
        slope = sl_ref[g]
        s1, s2, s3 = split3(slope * float(KEY_TILE))
        r1, r2, r3 = split3(slope)
        for qb in range(nqb):
            lanes = slice(qb * Q_BLOCK, (qb + 1) * Q_BLOCK)
            for j in range(NSA_GROUP):
                r0 = (g * NSA_GROUP + j) * NSA_DH
                q_ref[g, qb, 0:NSA_DH, j * Q_BLOCK:(j + 1) * Q_BLOCK] = pt[r0:r0 + NSA_DH, lanes].astype(BF16)
            tq = ((i * nqb + qb) * Q_BLOCK).astype(F32) + ql
            o1, o2, o3 = split3(-(slope * tq))
            aug = jnp.zeros((16, QL), F32)
            for k, term in enumerate((s1, s2, s3, o1, o2, o3, r1, r2, r3)):
                aug = jnp.where(r16 == k, term, aug)
            q_ref[g, qb, NSA_DH:NSA_DH + 16, :] = aug.astype(BF16)
            q_ref[g, qb, NSA_DH + 16:AUG, :] = jnp.zeros((AUG - NSA_DH - 16, QL), BF16)
            for c in range(3):
                for j in range(NSA_GROUP):
                    r0 = qw + (g * 3 + c) * NSA_GROUP + j
                    gl_ref[g, qb, c:c + 1, j * Q_BLOCK:(j + 1) * Q_BLOCK] = pt[r0:r0 + 1, lanes]


def _nsa_inproj(x, nw, wt, slopes, nqb=2):
    t, d = x.shape
    nq = t // Q_BLOCK
    g = NSA_KV_HEADS
    tm = nqb * Q_BLOCK
    return pl.pallas_call(
        functools.partial(_nsa_inproj_kernel, nqb=nqb),
        grid=(t // tm,),
        in_specs=[pl.BlockSpec((tm, d), lambda i: (i, 0)),
                  pl.BlockSpec((1, d), lambda i: (0, 0)),
                  pl.BlockSpec(wt.shape, lambda i: (0, 0)),
                  pl.BlockSpec(slopes.shape, lambda i: (0, 0, 0))],
        out_specs=[pl.BlockSpec((g, nqb, AUG, QL), lambda i: (0, i, 0, 0)),
                   pl.BlockSpec((g, nqb, 3, QL), lambda i: (0, i, 0, 0))],
        out_shape=[jax.ShapeDtypeStruct((g, nq, AUG, QL), BF16),
                   jax.ShapeDtypeStruct((g, nq, 3, QL), F32)],
        compiler_params=_params(1),
        name="nsa_inproj",
    )(x, nw.reshape(1, d), wt, slopes)


def _compress_kernel(hf_ref, pe_ref, w1_ref, w2_ref, o_ref):
    x = hf_ref[0, 0]
    nc, half = x.shape
    pe_lo = pe_ref[0, 0:1, :]
    pe_hi = pe_ref[0, 1:2, :]
    w1_lo = w1_ref[0, 0:half, :]
    w1_hi = w1_ref[0, half:2 * half, :]
    u = _dot((x + pe_lo).astype(BF16), w1_lo)
    vv = _dot((x + pe_hi).astype(BF16), w1_hi)
    nxt = pltpu.roll(vv, nc - 1, 0)
    pad = _dot(jnp.broadcast_to(pe_hi, (8, half)).astype(BF16), w1_hi)[0:1]
    row = lax.broadcasted_iota(jnp.int32, vv.shape, 0)
    pre = u + jnp.where(row == nc - 1, pad, nxt)
    hid = 0.5 * pre * (1.0 + jnp.tanh(0.7978845608028654 * (pre + 0.044715 * (pre * pre * pre))))
    o_ref[0, 0] = _dot(hid.astype(BF16), w2_ref[0])


def _compress(hf, pe, w1, w2):
    s, g, nc, half = hf.shape
    return pl.pallas_call(
        _compress_kernel,
        grid=(s, g),
        in_specs=[pl.BlockSpec((1, 1, nc, half), lambda i, j: (i, j, 0, 0)),
                  pl.BlockSpec((1, 2, half), lambda i, j: (i, 0, 0)),
                  pl.BlockSpec((1,) + w1.shape[1:], lambda i, j: (i, 0, 0)),
                  pl.BlockSpec((1,) + w2.shape[1:], lambda i, j: (i, 0, 0))],
        out_specs=pl.BlockSpec((1, 1, nc, w2.shape[2]), lambda i, j: (i, j, 0, 0)),
        out_shape=jax.ShapeDtypeStruct((s, g, nc, w2.shape[2]), F32),
        compiler_params=_params(2),
        name="compress",
    )(hf, pe, w1, w2)


def _nsa_cmp_kernel(q_ref, kc_ref, vct_ref, cb_ref, oc_ref, sel_ref, w_ref, p_ref, *, ns, topk):
    qi = pl.program_id(1)
    t0f = (qi * Q_BLOCK).astype(F32)
    qa = q_ref[0, 0]
    ratio = SLC_BLOCK // CMP_STRIDE

    def attend(nv):
        cb = cb_ref[0:nv, :]
        mx = jnp.full((1, QL), NEG_INF, F32)
        for u in range(ratio):
            s = _dot(kc_ref[0, u * ns:u * ns + nv, :], qa)
            s = jnp.where(cb >= float(CMP_STRIDE * u + CMP_BLOCK - 1) - t0f, s, NEG_INF)
            p_ref[u * ns:u * ns + nv, :] = s
            mx = jnp.maximum(mx, jnp.max(s, axis=0, keepdims=True))
        valid = mx > 0.5 * NEG_INF
        n_i = lax.broadcasted_iota(jnp.int32, (nv, Q_BLOCK), 0)
        ql = lax.broadcasted_iota(jnp.int32, (nv, Q_BLOCK), 1)
        acc = jnp.zeros((VROWS, QL), F32)
        head_pool = [None] * NSA_GROUP
        for u in range(ratio):
            p = jnp.exp(p_ref[u * ns:u * ns + nv, :] - mx)
            acc = acc + _dot(vct_ref[0, :, u * ns:u * ns + nv], p.astype(BF16))
            for j in range(NSA_GROUP):
                pj = p[:, j * Q_BLOCK:(j + 1) * Q_BLOCK]
                head_pool[j] = pj if head_pool[j] is None else head_pool[j] + pj
                if u == ratio - 1:
                    head_pool[j] = head_pool[j] + jnp.where(n_i >= 1, pltpu.roll(pj, 1, 0), 0.0)
        inv = jnp.where(valid, 1.0 / acc[NSA_DH:NSA_DH + 1], 0.0)
        oc_ref[0, 0] = acc[0:NSA_DH] * inv
        pooled = head_pool[0] * inv[:, 0:Q_BLOCK]
        for j in range(1, NSA_GROUP):
            pooled = pooled + head_pool[j] * inv[:, j * Q_BLOCK:(j + 1) * Q_BLOCK]
        cur = 2 * qi + (ql >= SLC_BLOCK).astype(jnp.int32)
        ok = n_i <= cur
        forced = ok & ((n_i == 0) | (n_i == cur) | (n_i == cur - 1))
        cand = jnp.where(ok & jnp.logical_not(forced), pooled, NEG_INF)
        n_f = n_i.astype(F32)
        score = cand
        for _ in range(topk - 3):
            best = jnp.max(score, axis=0, keepdims=True)
            idx = jnp.min(jnp.where(score == best, n_f, float(ns)), axis=0, keepdims=True)
            score = jnp.where(n_f == idx, -jnp.inf, score)
        picked = (score == -jnp.inf) & (cand > 0.5 * NEG_INF)
        sel = jnp.where(forced | picked, 1.0, 0.0)
        sel_ref[0, 0, 0:nv, :] = sel
        if nv < ns:
            sel_ref[0, 0, nv:ns, :] = jnp.zeros((ns - nv, Q_BLOCK), F32)

        any_q = jnp.max(sel, axis=1, keepdims=True)
        wk = lax.broadcasted_iota(jnp.int32, (nv, 128), 1)
        wn = lax.broadcasted_iota(jnp.int32, (nv, 128), 0)
        weight = jnp.where((wn // WORD_BITS) == wk,
                           jnp.left_shift(1, wn & (WORD_BITS - 1)).astype(F32), 0.0)
        w_ref[0, 0] = jnp.sum(any_q * weight, axis=0, keepdims=True).astype(jnp.int32)

    n_cls = 4
    step = ns // n_cls
    need = 2 * qi + 2
    for k in range(n_cls):
        lo, hi = k * step, (k + 1) * step

        @pl.when((need > lo) & (need <= hi))
        def _():
            attend(hi)


def _nsa_cmp(qa, kc, vct, cbase):
    g, nq, _, _ = qa.shape
    dh = NSA_DH
    ncmp = kc.shape[1]
    ns = ncmp // (SLC_BLOCK // CMP_STRIDE)
    topk = min(SLC_TOPK, ns)
    return pl.pallas_call(
        functools.partial(_nsa_cmp_kernel, ns=ns, topk=topk),
        grid=(g, nq),
        in_specs=[pl.BlockSpec((1, 1, AUG, QL), lambda i, j: (i, j, 0, 0)),
                  pl.BlockSpec((1, ncmp, AUG), lambda i, j: (i, 0, 0)),
                  pl.BlockSpec((1, VROWS, ncmp), lambda i, j: (i, 0, 0)),
                  pl.BlockSpec((ns, QL), lambda i, j: (0, 0))],
        out_specs=[pl.BlockSpec((1, 1, dh, QL), lambda i, j: (i, j, 0, 0)),
                   pl.BlockSpec((1, 1, ns, Q_BLOCK), lambda i, j: (i, j, 0, 0)),
                   pl.BlockSpec((1, 1, 1, 128), lambda i, j: (i, j, 0, 0))],
        out_shape=[jax.ShapeDtypeStruct((g, nq, dh, QL), F32),
                   jax.ShapeDtypeStruct((g, nq, ns, Q_BLOCK), F32),
                   jax.ShapeDtypeStruct((g, nq, 1, 128), jnp.int32)],
        scratch_shapes=[pltpu.VMEM((ncmp, QL), F32)],
        compiler_params=_params(2),
        name="nsa_cmp",
    )(qa, kc, vct, cbase)


def _nsa_sw_kernel(words_ref, q_ref, gl_ref, oc_ref, sel_ref, base_ref, ks_ref, vst_ref,
                   kw_ref, vwt_ref, o_ref, list_ref, m_ref, acc_ref, *, nq, nwords):
    g = pl.program_id(0)
    qi = pl.program_id(1)
    t0 = qi * Q_BLOCK
    qa = q_ref[0, 0]
    half = SLC_BLOCK
    tile_base = base_ref[...]

    for u in range(SEL_STATIC):
        list_ref[u] = 0
    wbase = (g * nq + qi) * nwords
    per_word = WORD_BITS // 2

    def word_body(wi, cnt):
        word = words_ref[wbase + wi]

        def scan(cnt):
            for bp in range(per_word):
                m = wi * per_word + bp
                bits = lax.shift_right_logical(word, 2 * bp) & 3
                list_ref[cnt] = m
                cnt = cnt + jnp.logical_and(bits != 0, m < qi).astype(jnp.int32)
            return cnt

        return lax.cond(word != 0, scan, lambda c: c, cnt)

    cnt = lax.fori_loop(0, (qi + per_word - 1) // per_word, word_body, jnp.int32(0))
    for k in range(1, SEL_CHUNK):
        list_ref[jnp.minimum(cnt + k, nq - 1)] = 0

    def probs(s, mx):
        return jnp.exp((s - mx).astype(BF16))

    def sel_mask(ra, rb):
        mk = jnp.concatenate([jnp.broadcast_to(ra, (half, Q_BLOCK)), jnp.broadcast_to(rb, (half, Q_BLOCK))], axis=0)
        return jnp.concatenate([mk] * NSA_GROUP, axis=1) > 0.5

    def listed_tile(slot):
        live = slot < cnt
        m = jnp.where(live, list_ref[jnp.minimum(slot, nq - 1)], 0)
        keep = jnp.where(live, 1.0, 0.0)
        pos = pl.multiple_of(m * KEY_TILE, KEY_TILE)
        mask = sel_mask(sel_ref[0, 0, pl.ds(2 * m, 1), :] * keep, sel_ref[0, 0, pl.ds(2 * m + 1, 1), :] * keep)
        return ks_ref[0, pl.ds(pos, KEY_TILE), :], vst_ref[0, :, pl.ds(pos, KEY_TILE)], mask

    def masked_scores(k_tiles, masks):
        sc = _dot(jnp.concatenate(k_tiles, axis=0), qa)
        tiles, mx = [], None
        for u, mask in enumerate(masks):
            s = jnp.where(mask, sc[u * KEY_TILE:(u + 1) * KEY_TILE, :], NEG_INF)
            mu = jnp.max(s, axis=0, keepdims=True)
            mx = mu if mx is None else jnp.maximum(mx, mu)
            tiles.append(s)
        return tiles, mx

    def pv(s_tiles, v_tiles, mx):
        acc = None
        for u in range(0, len(s_tiles), 2):
            d = _dot(jnp.concatenate(v_tiles[u:u + 2], axis=1),
                     probs(jnp.concatenate(s_tiles[u:u + 2], axis=0), mx))
            acc = d if acc is None else acc + d
        return acc

    n_wt = (WINDOW + Q_BLOCK) // KEY_TILE
    wk, wv, wmask = [], [], []
    for u in range(n_wt):
        pos_raw = t0 - WINDOW + u * KEY_TILE
        pos = pl.multiple_of(jnp.maximum(pos_raw, 0), KEY_TILE)
        wk.append(kw_ref[0, pl.ds(pos, KEY_TILE), :])
        wv.append(vwt_ref[0, :, pl.ds(pos, KEY_TILE)])
        live = pos_raw >= 0
        if u == 0:
            wmask.append(jnp.logical_and(tile_base < 0, live))
        elif u == n_wt - 1:
            wmask.append(tile_base >= 0)
        else:
            wmask.append(live)

    posd = pl.multiple_of(t0, KEY_TILE)
    diag_mask = sel_mask(sel_ref[0, 0, pl.ds(2 * qi, 1), :], sel_ref[0, 0, pl.ds(2 * qi + 1, 1), :]) & (tile_base >= 0)
    listed = [listed_tile(jnp.int32(u)) for u in range(SEL_STATIC)]
    stiles, m0 = masked_scores([ks_ref[0, pl.ds(posd, KEY_TILE), :]] + [t[0] for t in listed],
                               [diag_mask] + [t[2] for t in listed])
    wtiles, mw = masked_scores(wk, wmask)
    m_ref[...] = m0
    acc_ref[...] = pv(stiles, [vst_ref[0, :, pl.ds(posd, KEY_TILE)]] + [t[1] for t in listed], m0)
    accw = pv(wtiles, wv, mw)
    o_win = accw[0:NSA_DH] * (1.0 / accw[NSA_DH:NSA_DH + 1])

    def chunk_body(ci, carry):
        listed = [listed_tile(SEL_STATIC + ci * SEL_CHUNK + u) for u in range(SEL_CHUNK)]
        tiles, cmax = masked_scores([t[0] for t in listed], [t[2] for t in listed])
        m_old = m_ref[...]
        m_new = jnp.maximum(m_old, cmax)
        acc_ref[...] = jnp.exp(m_old - m_new) * acc_ref[...] + pv(tiles, [t[1] for t in listed], m_new)
        m_ref[...] = m_new
        return carry

    lax.fori_loop(0, (jnp.maximum(cnt - SEL_STATIC, 0) + SEL_CHUNK - 1) // SEL_CHUNK, chunk_body, 0)
    o_sel = acc_ref[0:NSA_DH, :] * (1.0 / acc_ref[NSA_DH:NSA_DH + 1, :])

    gl = gl_ref[0, 0]
    gate = 1.0 / (1.0 + jnp.exp(-gl))
    out_t = gate[0:1] * oc_ref[0, 0] + gate[1:2] * o_sel + gate[2:3] * o_win
    halves = []
    for j in range(0, NSA_GROUP, 2):
        pair = jnp.concatenate([out_t[:, j * Q_BLOCK:(j + 1) * Q_BLOCK],
                                out_t[:, (j + 1) * Q_BLOCK:(j + 2) * Q_BLOCK]], axis=0)
        halves.append(pair.T)
    o_ref[...] = jnp.concatenate(halves, axis=1).astype(o_ref.dtype)


def _nsa_sw(words, qa, glog, oct_, selt, base, ks, vst, kw, vwt):
    g, nq, _, _ = qa.shape
    dh = NSA_DH
    t = ks.shape[1]
    ns = selt.shape[2]
    nwords = words.shape[0] // (g * nq)
    grid_spec = pltpu.PrefetchScalarGridSpec(
        num_scalar_prefetch=1,
        grid=(g, nq),
        in_specs=[pl.BlockSpec((1, 1, AUG, QL), lambda i, j, w: (i, j, 0, 0)),
                  pl.BlockSpec((1, 1, 3, QL), lambda i, j, w: (i, j, 0, 0)),
                  pl.BlockSpec((1, 1, dh, QL), lambda i, j, w: (i, j, 0, 0)),
                  pl.BlockSpec((1, 1, ns, Q_BLOCK), lambda i, j, w: (i, j, 0, 0)),
                  pl.BlockSpec((KEY_TILE, QL), lambda i, j, w: (0, 0)),
                  pl.BlockSpec((1, t, AUG), lambda i, j, w: (i, 0, 0)),
                  pl.BlockSpec((1, VROWS, t), lambda i, j, w: (i, 0, 0)),
                  pl.BlockSpec((1, t, AUG), lambda i, j, w: (i, 0, 0)),
                  pl.BlockSpec((1, VROWS, t), lambda i, j, w: (i, 0, 0))],
        out_specs=pl.BlockSpec((Q_BLOCK, NSA_GROUP * dh), lambda i, j, w: (j, i)),
        scratch_shapes=[pltpu.SMEM((nq,), jnp.int32),
                        pltpu.VMEM((1, QL), F32), pltpu.VMEM((VROWS, QL), F32)],
    )
    return pl.pallas_call(
        functools.partial(_nsa_sw_kernel, nq=nq, nwords=nwords),
        grid_spec=grid_spec,
        out_shape=jax.ShapeDtypeStruct((t, g * NSA_GROUP * dh), BF16),
        compiler_params=_params(2),
        name="nsa_sw",
    )(words, qa, glog, oct_, selt, base, ks, vst, kw, vwt)


def kernel(x, a_norm_w, a_w_in, a_gnorm_w, a_w_out, a_lower_bounds, kv_norm_w, kv_w, cmp_pe_k, cmp_w1_k,
           cmp_w2_k, cmp_pe_v, cmp_w1_v, cmp_w2_v, b_norm_w, b_w_in, b_w_out, mlp_norm_w, mlp_w_up,
           mlp_w_down, final_norm_w):
    bsz, t, d = x.shape
    assert bsz == 1 and t % Q_BLOCK == 0 and t >= WINDOW + Q_BLOCK
    n_a = a_w_in.shape[0]
    n_b = b_w_in.shape[0]
    nq = t // Q_BLOCK
    xs = x[0]

    for layer in range(n_a):
        proj = _norm_matmul(xs, a_norm_w[layer], a_w_in[layer].astype(BF16))
        o = _hgrn(proj, a_lower_bounds, a_gnorm_w[layer], layer)
        xs = _proj_mlp(xs, o, a_w_out[layer].astype(BF16), mlp_norm_w[layer],
                       mlp_w_up[layer].astype(BF16), mlp_w_down[layer].astype(BF16),
                       final_norm_w, final=False)

    gd = NSA_KV_HEADS * NSA_DH
    kvw = kv_w.reshape(d, N_KV_STREAMS, NSA_KV_HEADS, NSA_DH)
    wc = kvw[:, 0:2].reshape(d, 2 * gd).astype(BF16)
    wk = jnp.pad(jnp.stack([kvw[:, 2], kvw[:, 4]], axis=1), ((0, 0), (0, 0), (0, 0), (0, AUG - NSA_DH)))
    wk = wk.reshape(d, 2 * NSA_KV_HEADS * AUG).astype(BF16)
    wvt = jnp.stack([kvw[:, 3], kvw[:, 5]], axis=1).reshape(d, 2 * gd).T.astype(BF16)
    kc_in, vc_in, ks, kw, vst, vwt = _kv_proj(xs, kv_norm_w, wc, wk, wvt)

    ncmp = t // CMP_STRIDE
    half = CMP_STRIDE * NSA_DH
    hf = jnp.stack([kc_in.reshape(NSA_KV_HEADS, ncmp, half), vc_in.reshape(NSA_KV_HEADS, ncmp, half)])
    pe = jnp.stack([cmp_pe_k.reshape(2, half), cmp_pe_v.reshape(2, half)])
    w1 = jnp.stack([cmp_w1_k, cmp_w1_v]).astype(BF16)
    w2 = jnp.stack([cmp_w2_k, cmp_w2_v]).astype(BF16)
    cmp_out = _compress(hf, pe, w1, w2)
    ratio = SLC_BLOCK // CMP_STRIDE
    ns = ncmp // ratio
    perm = cmp_out.reshape(2, NSA_KV_HEADS, ns, ratio, NSA_DH).transpose(0, 1, 3, 2, 4).reshape(
        2, NSA_KV_HEADS, ncmp, NSA_DH)
    crow = jnp.arange(ncmp)
    cmp_end = SLC_BLOCK * (crow % ns) + CMP_STRIDE * (crow // ns) + (CMP_BLOCK - 1)
    aug_cols = jnp.stack([cmp_end // KEY_TILE] * 3 + [jnp.ones_like(cmp_end)] * 3 + [cmp_end % KEY_TILE] * 3,
                         axis=1).astype(F32)
    aug_cols = jnp.pad(aug_cols, ((0, 0), (0, AUG - NSA_DH - aug_cols.shape[1])))
    kc = jnp.concatenate([perm[0], jnp.broadcast_to(aug_cols, (NSA_KV_HEADS, ncmp, AUG - NSA_DH))],
                         axis=2).astype(BF16)
    ones_rows = jnp.zeros((NSA_KV_HEADS, VROWS - NSA_DH, ncmp), F32).at[:, 0, :].set(1.0)
    vct = jnp.concatenate([perm[1].transpose(0, 2, 1), ones_rows], axis=1).astype(BF16)

    slopes = jnp.exp2(-8.0 * jnp.arange(1, NSA_HEADS + 1, dtype=F32) / NSA_HEADS)
    slopes = jnp.repeat(slopes.reshape(NSA_KV_HEADS, 1, NSA_GROUP), Q_BLOCK, axis=2)
    q_off = jnp.tile(jnp.arange(Q_BLOCK, dtype=F32), NSA_GROUP)[None, :]
    base = q_off - jnp.arange(KEY_TILE, dtype=F32)[:, None]
    cbase = q_off - float(SLC_BLOCK) * jnp.arange(ns, dtype=F32)[:, None]

    qw = NSA_HEADS * NSA_DH
    n_gate = b_w_in.shape[2] - qw
    pad_rows = (-(qw + n_gate)) % 128
    nwords = ns // WORD_BITS
    for b in range(n_b):
        w_q = b_w_in[b][:, 0:qw] * (NSA_DH ** -0.5)
        w_g = b_w_in[b][:, qw:].reshape(d, NSA_KV_HEADS, NSA_GROUP, 3).transpose(0, 1, 3, 2).reshape(d, n_gate)
        wt_in = jnp.pad(jnp.concatenate([w_q, w_g], axis=1), ((0, 0), (0, pad_rows))).T.astype(BF16)
        qa, glog = _nsa_inproj(xs, b_norm_w[b], wt_in, slopes)
        oct_, selt, words = _nsa_cmp(qa, kc, vct, cbase)
        words = words[:, :, 0, 0:nwords].reshape(-1)
        o = _nsa_sw(words, qa, glog, oct_, selt, base, ks, vst, kw, vwt)
        xs = _proj_mlp(xs, o, b_w_out[b].astype(BF16), mlp_norm_w[n_a + b],
                       mlp_w_up[n_a + b].astype(BF16), mlp_w_down[n_a + b].astype(BF16),
                       final_norm_w, final=(b == n_b - 1))
    return xs[None]
```

```python
import functools

import jax
import jax.numpy as jnp
from jax import lax
from jax.experimental import pallas as pl
from jax.experimental.pallas import tpu as pltpu

F32 = jnp.float32
BF16 = jnp.bfloat16

NORM_EPS = 1e-6
NEG_INF = -1e30
GATE_FLOOR = 1e-30

HG_HEADS = 8
HG_DK = 128
HG_DV = 128
HG_CHUNK = 64
HG_STEP_CHUNKS = 4

NSA_HEADS = 16
NSA_KV_HEADS = 4
NSA_GROUP = NSA_HEADS // NSA_KV_HEADS
NSA_DH = 64
CMP_BLOCK = 32
CMP_STRIDE = 16
SLC_BLOCK = 64
SLC_TOPK = 16
WINDOW = 512
Q_BLOCK = 128
FORCE_BONUS = 1e4
N_KV_STREAMS = 6

KEY_TILE = 128
SEL_STATIC = 8
SEL_CHUNK = 4
SW_QBLOCKS = 2
CMP_QBLOCKS = 4
WORD_BITS = 16
QL = NSA_GROUP * Q_BLOCK
AUG = 128
VROWS = NSA_DH + 16

VMEM_LIMIT = 56 * 1024 * 1024


def _params(n_axes, vmem=VMEM_LIMIT):
    return pltpu.CompilerParams(dimension_semantics=("arbitrary",) * n_axes, vmem_limit_bytes=vmem)


def _nt(a, b):
    return lax.dot_general(a, b, (((1,), (1,)), ((), ())), preferred_element_type=F32)


def _tn(a, b):
    return lax.dot_general(a, b, (((0,), (0,)), ((), ())), preferred_element_type=F32)


def _dot(a, b):
    return jnp.dot(a, b, preferred_element_type=F32)


def _rmsnorm(x, w):
    ms = jnp.mean(x * x, axis=-1, keepdims=True)
    return x * lax.rsqrt(ms + NORM_EPS) * w


def _hgrn_kernel(x_ref, nw_ref, w_ref, lbp_ref, gw_ref, seg_ref, o_ref, st_ref, proj_ref, *, layer, n_chunks):
    c = HG_CHUNK
    hk = HG_HEADS * HG_DK

    @pl.when(pl.program_id(0) == 0)
    def _():
        st_ref[...] = jnp.zeros_like(st_ref)

    proj_ref[...] = _dot(_rmsnorm(x_ref[...], nw_ref[...]).astype(BF16), w_ref[...])

    a = lbp_ref[...]
    e = jnp.exp(a - jnp.max(a, axis=0, keepdims=True))
    p = e / jnp.sum(e, axis=0, keepdims=True)
    cum = p[0:1]
    for i in range(1, layer + 1):
        cum = cum + p[i:i + 1]
    lb = cum - p[0:1]
    one_m = 1.0 - lb

    sub = lax.broadcasted_iota(jnp.int32, (8, hk), 0)
    ti = lax.broadcasted_iota(jnp.int32, (c, c), 0)
    si = lax.broadcasted_iota(jnp.int32, (c, c), 1)
    masks = [(((ti >> j) & 1) == 1) & (((si >> j) & 1) == 0) & ((ti >> (j + 1)) == (si >> (j + 1)))
             for j in range(6)]
    diag = ti == si
    gw = gw_ref[...]

    for ch in range(n_chunks):
        rows = slice(ch * c, (ch + 1) * c)
        q = proj_ref[rows, 0:hk]
        fp = proj_ref[rows, hk:2 * hk]
        v = proj_ref[rows, 2 * hk:3 * hk]
        go = proj_ref[rows, 3 * hk:4 * hk]

        ea = jnp.exp(-jnp.abs(fp))
        r = 1.0 / (1.0 + ea)
        pos = fp >= 0
        f_gate = lb + one_m * jnp.where(pos, r, ea * r)
        kk = one_m * jnp.where(pos, ea * r, r)
        logf = jnp.log(jnp.maximum(f_gate, GATE_FLOOR))

        hi = logf.astype(BF16)
        rem = logf - hi.astype(F32)
        mid = rem.astype(BF16)
        lo = (rem - mid.astype(F32)).astype(BF16)
        sums = _dot(seg_ref[...], jnp.concatenate([hi, mid, lo], axis=0))
        b = sums[0:c]
        eb = jnp.exp(b[c - 1:c, :])

        xs = []
        for j in range(6):
            ex = jnp.exp(sums[c * (j + 1):c * (j + 2)])
            slabs = []
            for r0 in range(0, c, 8):
                if j < 3:
                    slabs.append(jnp.where(((sub >> j) & 1) == 1, q[r0:r0 + 8], kk[r0:r0 + 8]))
                else:
                    slabs.append(q[r0:r0 + 8] if (r0 >> j) & 1 else kk[r0:r0 + 8])
            xs.append((jnp.concatenate(slabs, axis=0) * ex).astype(BF16))

        qb = q.astype(BF16)
        kb = kk.astype(BF16)
        vb = v.astype(BF16)
        qs = (q * jnp.exp(b)).astype(BF16)
        khat = (kk * jnp.exp(sums[7 * c:8 * c])).astype(BF16)

        heads = [slice(h * HG_DK, (h + 1) * HG_DK) for h in range(HG_HEADS)]
        attn = [jnp.where(diag, _nt(qb[:, sl], kb[:, sl]), 0.0) for sl in heads]
        for j in range(6):
            for h, sl in enumerate(heads):
                xj = xs[j][:, sl]
                attn[h] = attn[h] + jnp.where(masks[j], _nt(xj, xj), 0.0)
        outs = []
        for h, sl in enumerate(heads):
            st = st_ref[h]
            outs.append(_dot(attn[h].astype(BF16), vb[:, sl]) + _nt(qs[:, sl], st.astype(BF16)))
            st_ref[h] = st * eb[:, sl] + _tn(vb[:, sl], khat[:, sl])
        for h, sl in enumerate(heads):
            g = go[:, sl]
            o_ref[rows, sl] = (_rmsnorm(outs[h], gw) * (g * (1.0 / (1.0 + jnp.exp(-g))))).astype(o_ref.dtype)


def _segment_matrix():
    c = HG_CHUNK
    t = jnp.arange(c)[:, None]
    i = jnp.arange(c)[None, :]
    parts = [i <= t]
    for j in range(6):
        m = 1 << j
        ref = t - (t % (2 * m)) + m
        parts.append(jnp.where(t >= ref, (i > ref) & (i <= t), (i > t) & (i <= ref)))
    parts.append(i > t)
    return jnp.tile(jnp.concatenate(parts, axis=0), (1, 3)).astype(BF16)


def _hgrn(x, norm_w, w_in, lb_param, gnorm_w, layer):
    t, d = x.shape
    hk = HG_HEADS * HG_DK
    hv = HG_HEADS * HG_DV
    n_layers = lb_param.shape[0]
    seg = _segment_matrix()
    tm = HG_STEP_CHUNKS * HG_CHUNK
    const = lambda i: (0, 0)
    return pl.pallas_call(
        functools.partial(_hgrn_kernel, layer=layer, n_chunks=HG_STEP_CHUNKS),
        grid=(t // tm,),
        in_specs=[pl.BlockSpec((tm, d), lambda i: (i, 0)),
                  pl.BlockSpec((1, d), const),
                  pl.BlockSpec(w_in.shape, const),
                  pl.BlockSpec((n_layers, hk), const),
                  pl.BlockSpec((1, HG_DV), const),
                  pl.BlockSpec(seg.shape, const)],
        out_specs=pl.BlockSpec((tm, hv), lambda i: (i, 0)),
        out_shape=jax.ShapeDtypeStruct((t, hv), BF16),
        scratch_shapes=[pltpu.VMEM((HG_HEADS, HG_DV, HG_DK), F32),
                        pltpu.VMEM((tm, 2 * hk + 2 * hv), F32)],
        compiler_params=_params(1),
        name="hgrn",
    )(x, norm_w.reshape(1, d), w_in, lb_param, gnorm_w.reshape(1, HG_DV), seg)


def _proj_mlp_kernel(x_ref, a_ref, wo_ref, nw_ref, wup_ref, wdn_ref, fnw_ref, o_ref, *, final, ff_tile):
    x1 = x_ref[...] + _dot(a_ref[...], wo_ref[...])
    h = _rmsnorm(x1, nw_ref[...]).astype(BF16)
    acc = x1
    d_ff = wup_ref.shape[1]
    for c0 in range(0, d_ff, ff_tile):
        u = jnp.maximum(_dot(h, wup_ref[:, c0:c0 + ff_tile]), 0.0)
        acc = acc + _dot((u * u).astype(BF16), wdn_ref[c0:c0 + ff_tile, :])
    if final:
        acc = _rmsnorm(acc, fnw_ref[...])
    o_ref[...] = acc


def _proj_mlp(x, a, wo, nw, wup, wdn, fnw, final, tm=256):
    t, d = x.shape
    d_ff = wup.shape[1]
    const = lambda i: (0, 0)
    return pl.pallas_call(
        functools.partial(_proj_mlp_kernel, final=final, ff_tile=min(1024, d_ff)),
        grid=(t // tm,),
        in_specs=[pl.BlockSpec((tm, d), lambda i: (i, 0)),
                  pl.BlockSpec((tm, a.shape[1]), lambda i: (i, 0)),
                  pl.BlockSpec(wo.shape, const),
                  pl.BlockSpec((1, d), const),
                  pl.BlockSpec(wup.shape, const),
                  pl.BlockSpec(wdn.shape, const),
                  pl.BlockSpec((1, d), const)],
        out_specs=pl.BlockSpec((tm, d), lambda i: (i, 0)),
        out_shape=jax.ShapeDtypeStruct((t, d), F32),
        compiler_params=_params(1),
        name="proj_mlp",
    )(x, a, wo, nw.reshape(1, d), wup, wdn, fnw.reshape(1, d))


def _kv_proj_kernel(x_ref, nw_ref, wc_ref, wk_ref, wvt_ref, kc_ref, vc_ref, ks_ref, kw_ref, vst_ref, vwt_ref):
    i = pl.program_id(0)
    tm = x_ref.shape[0]
    gd = NSA_KV_HEADS * NSA_DH
    h = _rmsnorm(x_ref[...], nw_ref[...]).astype(BF16)
    cmp_in = _dot(h, wc_ref[...])
    for g in range(NSA_KV_HEADS):
        kc_ref[g] = cmp_in[:, g * NSA_DH:(g + 1) * NSA_DH]
        vc_ref[g] = cmp_in[:, gd + g * NSA_DH:gd + (g + 1) * NSA_DH]
    kk = _dot(h, wk_ref[...])
    row = lax.broadcasted_iota(jnp.int32, (tm, AUG), 0)
    lane = lax.broadcasted_iota(jnp.int32, (tm, AUG), 1)
    tok = i * tm + row
    tile = (tok // KEY_TILE).astype(F32)
    offs = (tok % KEY_TILE).astype(F32)
    consts = jnp.where((lane >= NSA_DH) & (lane < NSA_DH + 3), tile,
                       jnp.where((lane >= NSA_DH + 3) & (lane < NSA_DH + 6), 1.0,
                                 jnp.where((lane >= NSA_DH + 6) & (lane < NSA_DH + 9), offs, 0.0)))
    for g in range(NSA_KV_HEADS):
        ks_ref[g] = (kk[:, g * AUG:(g + 1) * AUG] + consts).astype(BF16)
        kw_ref[g] = (kk[:, (NSA_KV_HEADS + g) * AUG:(NSA_KV_HEADS + g + 1) * AUG] + consts).astype(BF16)
    vt = _nt(wvt_ref[...], h)
    r16 = lax.broadcasted_iota(jnp.int32, (VROWS - NSA_DH, tm), 0)
    ones_blk = jnp.where(r16 == 0, 1.0, 0.0).astype(BF16)
    for g in range(NSA_KV_HEADS):
        vst_ref[g, 0:NSA_DH, :] = vt[g * NSA_DH:(g + 1) * NSA_DH, :].astype(BF16)
        vst_ref[g, NSA_DH:VROWS, :] = ones_blk
        vwt_ref[g, 0:NSA_DH, :] = vt[gd + g * NSA_DH:gd + (g + 1) * NSA_DH, :].astype(BF16)
        vwt_ref[g, NSA_DH:VROWS, :] = ones_blk


def _kv_proj(x, nw, wc, wk, wvt, tm=256):
    t, d = x.shape
    g = NSA_KV_HEADS
    const = lambda i: (0, 0)
    rows = lambda i: (0, i, 0)
    cols = lambda i: (0, 0, i)
    return pl.pallas_call(
        _kv_proj_kernel,
        grid=(t // tm,),
        in_specs=[pl.BlockSpec((tm, d), lambda i: (i, 0)),
                  pl.BlockSpec((1, d), const),
                  pl.BlockSpec(wc.shape, const),
                  pl.BlockSpec(wk.shape, const),
                  pl.BlockSpec(wvt.shape, const)],
        out_specs=[pl.BlockSpec((g, tm, NSA_DH), rows), pl.BlockSpec((g, tm, NSA_DH), rows),
                   pl.BlockSpec((g, tm, AUG), rows), pl.BlockSpec((g, tm, AUG), rows),
                   pl.BlockSpec((g, VROWS, tm), cols), pl.BlockSpec((g, VROWS, tm), cols)],
        out_shape=[jax.ShapeDtypeStruct((g, t, NSA_DH), F32), jax.ShapeDtypeStruct((g, t, NSA_DH), F32),
                   jax.ShapeDtypeStruct((g, t, AUG), BF16), jax.ShapeDtypeStruct((g, t, AUG), BF16),
                   jax.ShapeDtypeStruct((g, VROWS, t), BF16), jax.ShapeDtypeStruct((g, VROWS, t), BF16)],
        compiler_params=_params(1),
        name="kv_proj",
    )(x, nw.reshape(1, d), wc, wk, wvt)


def _nsa_inproj_kernel(x_ref, nw_ref, wt_ref, sl_ref, q_ref, gl_ref, *, nqb):
    i = pl.program_id(0)
    qw = NSA_HEADS * NSA_DH
    h = _rmsnorm(x_ref[...], nw_ref[...]).astype(BF16)
    pt = _nt(wt_ref[...], h)
    r16 = lax.broadcasted_iota(jnp.int32, (16, QL), 0)
    ql = (lax.broadcasted_iota(jnp.int32, (1, QL), 1) & (Q_BLOCK - 1)).astype(F32)

    def split3(v):
        a = v.astype(BF16).astype(F32)
        b = (v - a).astype(BF16).astype(F32)
        return a, b, v - a - b

    for g in range(NSA_KV_HEADS):
        slope = sl_ref[g]
        s1, s2, s3 = split3(slope * float(KEY_TILE))
        r1, r2, r3 = split3(slope)
        for qb in range(nqb):
            lanes = slice(qb * Q_BLOCK, (qb + 1) * Q_BLOCK)
            for j in range(NSA_GROUP):
                r0 = (g * NSA_GROUP + j) * NSA_DH
                q_ref[g, qb, 0:NSA_DH, j * Q_BLOCK:(j + 1) * Q_BLOCK] = pt[r0:r0 + NSA_DH, lanes].astype(BF16)
            tq = ((i * nqb + qb) * Q_BLOCK).astype(F32) + ql
            o1, o2, o3 = split3(-(slope * tq))
            aug = jnp.zeros((16, QL), F32)
            for k, term in enumerate((s1, s2, s3, o1, o2, o3, r1, r2, r3)):
                aug = jnp.where(r16 == k, term, aug)
            q_ref[g, qb, NSA_DH:NSA_DH + 16, :] = aug.astype(BF16)
            q_ref[g, qb, NSA_DH + 16:AUG, :] = jnp.zeros((AUG - NSA_DH - 16, QL), BF16)
            for c in range(3):
                for j in range(NSA_GROUP):
                    r0 = qw + (g * 3 + c) * NSA_GROUP + j
                    gl_ref[g, qb, c:c + 1, j * Q_BLOCK:(j + 1) * Q_BLOCK] = pt[r0:r0 + 1, lanes]


def _nsa_inproj(x, nw, wt, slopes, nqb=2):
    t, d = x.shape
    nq = t // Q_BLOCK
    g = NSA_KV_HEADS
    tm = nqb * Q_BLOCK
    return pl.pallas_call(
        functools.partial(_nsa_inproj_kernel, nqb=nqb),
        grid=(t // tm,),
        in_specs=[pl.BlockSpec((tm, d), lambda i: (i, 0)),
                  pl.BlockSpec((1, d), lambda i: (0, 0)),
                  pl.BlockSpec(wt.shape, lambda i: (0, 0)),
                  pl.BlockSpec(slopes.shape, lambda i: (0, 0, 0))],
        out_specs=[pl.BlockSpec((g, nqb, AUG, QL), lambda i: (0, i, 0, 0)),
                   pl.BlockSpec((g, nqb, 3, QL), lambda i: (0, i, 0, 0))],
        out_shape=[jax.ShapeDtypeStruct((g, nq, AUG, QL), BF16),
                   jax.ShapeDtypeStruct((g, nq, 3, QL), F32)],
        compiler_params=_params(1),
        name="nsa_inproj",
    )(x, nw.reshape(1, d), wt, slopes)


def _compress_kernel(hf_ref, pe_ref, w1_ref, w2_ref, o_ref):
    x = hf_ref[0, 0]
    nc, half = x.shape
    pe_lo = pe_ref[0, 0:1, :]
    pe_hi = pe_ref[0, 1:2, :]
    w1_lo = w1_ref[0, 0:half, :]
    w1_hi = w1_ref[0, half:2 * half, :]
    u = _dot((x + pe_lo).astype(BF16), w1_lo)
    vv = _dot((x + pe_hi).astype(BF16), w1_hi)
    nxt = pltpu.roll(vv, nc - 1, 0)
    pad = _dot(jnp.broadcast_to(pe_hi, (8, half)).astype(BF16), w1_hi)[0:1]
    row = lax.broadcasted_iota(jnp.int32, vv.shape, 0)
    pre = u + jnp.where(row == nc - 1, pad, nxt)
    hid = 0.5 * pre * (1.0 + jnp.tanh(0.7978845608028654 * (pre + 0.044715 * (pre * pre * pre))))
    o_ref[0, 0] = _dot(hid.astype(BF16), w2_ref[0])


def _compress(hf, pe, w1, w2):
    s, g, nc, half = hf.shape
    return pl.pallas_call(
        _compress_kernel,
        grid=(s, g),
        in_specs=[pl.BlockSpec((1, 1, nc, half), lambda i, j: (i, j, 0, 0)),
                  pl.BlockSpec((1, 2, half), lambda i, j: (i, 0, 0)),
                  pl.BlockSpec((1,) + w1.shape[1:], lambda i, j: (i, 0, 0)),
                  pl.BlockSpec((1,) + w2.shape[1:], lambda i, j: (i, 0, 0))],
        out_specs=pl.BlockSpec((1, 1, nc, w2.shape[2]), lambda i, j: (i, j, 0, 0)),
        out_shape=jax.ShapeDtypeStruct((s, g, nc, w2.shape[2]), F32),
        compiler_params=_params(2),
        name="compress",
    )(hf, pe, w1, w2)


def _nsa_cmp_kernel(q_ref, kc_ref, vct_ref, cb_ref, oc_ref, sel_ref, w_ref, p_ref, *, ns, topk, nqb):
    ratio = SLC_BLOCK // CMP_STRIDE
    qis = [pl.program_id(1) * nqb + qb for qb in range(nqb)]

    def attend(nv):
        cb = cb_ref[0:nv, :]
        n_i = lax.broadcasted_iota(jnp.int32, (nv, Q_BLOCK), 0)
        ql = lax.broadcasted_iota(jnp.int32, (nv, Q_BLOCK), 1)
        n_f = n_i.astype(F32)
        mxs = []
        for qb in range(nqb):
            qa = q_ref[0, qb]
            t0f = (qis[qb] * Q_BLOCK).astype(F32)
            mx = jnp.full((1, QL), NEG_INF, F32)
            for u in range(ratio):
                s = _dot(kc_ref[0, u * ns:u * ns + nv, :], qa)
                s = jnp.where(cb >= float(CMP_STRIDE * u + CMP_BLOCK - 1) - t0f, s, NEG_INF)
                p_ref[qb, u * ns:u * ns + nv, :] = s
                mx = jnp.maximum(mx, jnp.max(s, axis=0, keepdims=True))
            mxs.append(mx)
        cands, forceds = [], []
        for qb in range(nqb):
            mx = mxs[qb]
            valid = mx > 0.5 * NEG_INF
            acc = jnp.zeros((VROWS, QL), F32)
            head_pool = [None] * NSA_GROUP
            for u in range(ratio):
                p = jnp.exp(p_ref[qb, u * ns:u * ns + nv, :] - mx)
                acc = acc + _dot(vct_ref[0, :, u * ns:u * ns + nv], p.astype(BF16))
                for j in range(NSA_GROUP):
                    pj = p[:, j * Q_BLOCK:(j + 1) * Q_BLOCK]
                    head_pool[j] = pj if head_pool[j] is None else head_pool[j] + pj
                    if u == ratio - 1:
                        head_pool[j] = head_pool[j] + jnp.where(n_i >= 1, pltpu.roll(pj, 1, 0), 0.0)
            inv = jnp.where(valid, 1.0 / acc[NSA_DH:NSA_DH + 1], 0.0)
            oc_ref[0, qb] = acc[0:NSA_DH] * inv
            pooled = head_pool[0] * inv[:, 0:Q_BLOCK]
            for j in range(1, NSA_GROUP):
                pooled = pooled + head_pool[j] * inv[:, j * Q_BLOCK:(j + 1) * Q_BLOCK]
            cur = 2 * qis[qb] + (ql >= SLC_BLOCK).astype(jnp.int32)
            ok = n_i <= cur
            forced = ok & ((n_i == 0) | (n_i == cur) | (n_i == cur - 1))
            forceds.append(forced)
            cands.append(jnp.where(ok & jnp.logical_not(forced), pooled, NEG_INF))
        scores = list(cands)
        for _ in range(topk - 3):
            for qb in range(nqb):
                best = jnp.max(scores[qb], axis=0, keepdims=True)
                idx = jnp.min(jnp.where(scores[qb] == best, n_f, float(ns)), axis=0, keepdims=True)
                scores[qb] = jnp.where(n_f == idx, -jnp.inf, scores[qb])
        wk = lax.broadcasted_iota(jnp.int32, (nv, 128), 1)
        wn = lax.broadcasted_iota(jnp.int32, (nv, 128), 0)
        weight = jnp.where((wn // WORD_BITS) == wk,
                           jnp.left_shift(1, wn & (WORD_BITS - 1)).astype(F32), 0.0)
        for qb in range(nqb):
            picked = (scores[qb] == -jnp.inf) & (cands[qb] > 0.5 * NEG_INF)
            sel = jnp.where(forceds[qb] | picked, 1.0, 0.0)
            sel_ref[0, qb, 0:nv, :] = sel
            if nv < ns:
                sel_ref[0, qb, nv:ns, :] = jnp.zeros((ns - nv, Q_BLOCK), F32)
            any_q = jnp.max(sel, axis=1, keepdims=True)
            w_ref[0, qb] = jnp.sum(any_q * weight, axis=0, keepdims=True).astype(jnp.int32)

    n_cls = 4
    step = ns // n_cls
    need = 2 * qis[-1] + 2
    for k in range(n_cls):
        lo, hi = k * step, (k + 1) * step

        @pl.when((need > lo) & (need <= hi))
        def _():
            attend(hi)


def _nsa_cmp(qa, kc, vct, cbase):
    g, nq, _, _ = qa.shape
    dh = NSA_DH
    ncmp = kc.shape[1]
    ns = ncmp // (SLC_BLOCK // CMP_STRIDE)
    topk = min(SLC_TOPK, ns)
    nqb = CMP_QBLOCKS
    return pl.pallas_call(
        functools.partial(_nsa_cmp_kernel, ns=ns, topk=topk, nqb=nqb),
        grid=(g, nq // nqb),
        in_specs=[pl.BlockSpec((1, nqb, AUG, QL), lambda i, j: (i, j, 0, 0)),
                  pl.BlockSpec((1, ncmp, AUG), lambda i, j: (i, 0, 0)),
                  pl.BlockSpec((1, VROWS, ncmp), lambda i, j: (i, 0, 0)),
                  pl.BlockSpec((ns, QL), lambda i, j: (0, 0))],
        out_specs=[pl.BlockSpec((1, nqb, dh, QL), lambda i, j: (i, j, 0, 0)),
                   pl.BlockSpec((1, nqb, ns, Q_BLOCK), lambda i, j: (i, j, 0, 0)),
                   pl.BlockSpec((1, nqb, 1, 128), lambda i, j: (i, j, 0, 0))],
        out_shape=[jax.ShapeDtypeStruct((g, nq, dh, QL), F32),
                   jax.ShapeDtypeStruct((g, nq, ns, Q_BLOCK), F32),
                   jax.ShapeDtypeStruct((g, nq, 1, 128), jnp.int32)],
        scratch_shapes=[pltpu.VMEM((nqb, ncmp, QL), F32)],
        compiler_params=_params(2),
        name="nsa_cmp",
    )(qa, kc, vct, cbase)


def _nsa_sw_kernel(words_ref, q_ref, gl_ref, oc_ref, sel_ref, base_ref, ks_ref, vst_ref,
                   kw_ref, vwt_ref, o_ref, list_ref, m_ref, acc_ref, *, nq, nwords, nqb):
    g = pl.program_id(0)
    step = pl.program_id(1)
    half = SLC_BLOCK
    tile_base = base_ref[...]
    per_word = WORD_BITS // 2

    def list_tiles(qb, qi):
        off = qb * nq
        for u in range(SEL_STATIC):
            list_ref[off + u] = 0
        wbase = (g * nq + qi) * nwords

        def word_body(wi, cnt):
            word = words_ref[wbase + wi]

            def scan(cnt):
                for bp in range(per_word):
                    m = wi * per_word + bp
                    bits = lax.shift_right_logical(word, 2 * bp) & 3
                    list_ref[off + cnt] = m
                    cnt = cnt + jnp.logical_and(bits != 0, m < qi).astype(jnp.int32)
                return cnt

            return lax.cond(word != 0, scan, lambda c: c, cnt)

        cnt = lax.fori_loop(0, (qi + per_word - 1) // per_word, word_body, jnp.int32(0))
        for k in range(1, SEL_CHUNK):
            list_ref[off + jnp.minimum(cnt + k, nq - 1)] = 0
        return cnt

    def probs(s, mx):
        return jnp.exp((s - mx).astype(BF16))

    def sel_mask(ra, rb):
        mk = jnp.concatenate([jnp.broadcast_to(ra, (half, Q_BLOCK)), jnp.broadcast_to(rb, (half, Q_BLOCK))], axis=0)
        return jnp.concatenate([mk] * NSA_GROUP, axis=1) > 0.5

    def listed_tile(qb, cnt, slot):
        live = slot < cnt
        m = jnp.where(live, list_ref[qb * nq + jnp.minimum(slot, nq - 1)], 0)
        keep = jnp.where(live, 1.0, 0.0)
        pos = pl.multiple_of(m * KEY_TILE, KEY_TILE)
        mask = sel_mask(sel_ref[0, qb, pl.ds(2 * m, 1), :] * keep, sel_ref[0, qb, pl.ds(2 * m + 1, 1), :] * keep)
        return ks_ref[0, pl.ds(pos, KEY_TILE), :], vst_ref[0, :, pl.ds(pos, KEY_TILE)], mask

    def masked_scores(qa, k_tiles, masks):
        sc = _dot(jnp.concatenate(k_tiles, axis=0), qa)
        tiles, mx = [], None
        for u, mask in enumerate(masks):
            s = jnp.where(mask, sc[u * KEY_TILE:(u + 1) * KEY_TILE, :], NEG_INF)
            mu = jnp.max(s, axis=0, keepdims=True)
            mx = mu if mx is None else jnp.maximum(mx, mu)
            tiles.append(s)
        return tiles, mx

    def pv(s_tiles, v_tiles, mx):
        acc = None
        for u in range(0, len(s_tiles), 2):
            d = _dot(jnp.concatenate(v_tiles[u:u + 2], axis=1),
                     probs(jnp.concatenate(s_tiles[u:u + 2], axis=0), mx))
            acc = d if acc is None else acc + d
        return acc

    qis = [step * nqb + qb for qb in range(nqb)]
    cnts = [list_tiles(qb, qis[qb]) for qb in range(nqb)]

    n_wt = (WINDOW + Q_BLOCK) // KEY_TILE
    staged = []
    for qb in range(nqb):
        qi, cnt = qis[qb], cnts[qb]
        t0 = qi * Q_BLOCK
        qa = q_ref[0, qb]
        wk, wv, wmask = [], [], []
        for u in range(n_wt):
            pos_raw = t0 - WINDOW + u * KEY_TILE
            pos = pl.multiple_of(jnp.maximum(pos_raw, 0), KEY_TILE)
            wk.append(kw_ref[0, pl.ds(pos, KEY_TILE), :])
            wv.append(vwt_ref[0, :, pl.ds(pos, KEY_TILE)])
            live = pos_raw >= 0
            if u == 0:
                wmask.append(jnp.logical_and(tile_base < 0, live))
            elif u == n_wt - 1:
                wmask.append(tile_base >= 0)
            else:
                wmask.append(live)
        posd = pl.multiple_of(t0, KEY_TILE)
        diag_mask = sel_mask(sel_ref[0, qb, pl.ds(2 * qi, 1), :],
                             sel_ref[0, qb, pl.ds(2 * qi + 1, 1), :]) & (tile_base >= 0)
        listed = [listed_tile(qb, cnt, jnp.int32(u)) for u in range(SEL_STATIC)]
        stiles, m0 = masked_scores(qa, [ks_ref[0, pl.ds(posd, KEY_TILE), :]] + [t[0] for t in listed],
                                   [diag_mask] + [t[2] for t in listed])
        wtiles, mw = masked_scores(qa, wk, wmask)
        staged.append((stiles, m0, [vst_ref[0, :, pl.ds(posd, KEY_TILE)]] + [t[1] for t in listed], wtiles, mw, wv))

    o_wins = []
    for qb in range(nqb):
        stiles, m0, svals, wtiles, mw, wv = staged[qb]
        m_ref[qb] = m0
        acc_ref[qb] = pv(stiles, svals, m0)
        accw = pv(wtiles, wv, mw)
        o_wins.append(accw[0:NSA_DH] * (1.0 / accw[NSA_DH:NSA_DH + 1]))

    for qb in range(nqb):
        cnt = cnts[qb]
        qa = q_ref[0, qb]

        def chunk_body(ci, carry, qb=qb, cnt=cnt, qa=qa):
            listed = [listed_tile(qb, cnt, SEL_STATIC + ci * SEL_CHUNK + u) for u in range(SEL_CHUNK)]
            tiles, cmax = masked_scores(qa, [t[0] for t in listed], [t[2] for t in listed])
            m_old = m_ref[qb]
            m_new = jnp.maximum(m_old, cmax)
            acc_ref[qb] = jnp.exp(m_old - m_new) * acc_ref[qb] + pv(tiles, [t[1] for t in listed], m_new)
            m_ref[qb] = m_new
            return carry

        lax.fori_loop(0, (jnp.maximum(cnt - SEL_STATIC, 0) + SEL_CHUNK - 1) // SEL_CHUNK, chunk_body, 0)
        o_sel = acc_ref[qb, 0:NSA_DH, :] * (1.0 / acc_ref[qb, NSA_DH:NSA_DH + 1, :])

        gl = gl_ref[0, qb]
        gate = 1.0 / (1.0 + jnp.exp(-gl))
        out_t = gate[0:1] * oc_ref[0, qb] + gate[1:2] * o_sel + gate[2:3] * o_wins[qb]
        halves = []
        for j in range(0, NSA_GROUP, 2):
            pair = jnp.concatenate([out_t[:, j * Q_BLOCK:(j + 1) * Q_BLOCK],
                                    out_t[:, (j + 1) * Q_BLOCK:(j + 2) * Q_BLOCK]], axis=0)
            halves.append(pair.T)
        o_ref[qb * Q_BLOCK:(qb + 1) * Q_BLOCK, :] = jnp.concatenate(halves, axis=1).astype(o_ref.dtype)


def _nsa_sw(words, qa, glog, oct_, selt, base, ks, vst, kw, vwt):
    g, nq, _, _ = qa.shape
    dh = NSA_DH
    t = ks.shape[1]
    ns = selt.shape[2]
    nwords = words.shape[0] // (g * nq)
    nqb = SW_QBLOCKS
    grid_spec = pltpu.PrefetchScalarGridSpec(
        num_scalar_prefetch=1,
        grid=(g, nq // nqb),
        in_specs=[pl.BlockSpec((1, nqb, AUG, QL), lambda i, j, w: (i, j, 0, 0)),
                  pl.BlockSpec((1, nqb, 3, QL), lambda i, j, w: (i, j, 0, 0)),
                  pl.BlockSpec((1, nqb, dh, QL), lambda i, j, w: (i, j, 0, 0)),
                  pl.BlockSpec((1, nqb, ns, Q_BLOCK), lambda i, j, w: (i, j, 0, 0)),
                  pl.BlockSpec((KEY_TILE, QL), lambda i, j, w: (0, 0)),
                  pl.BlockSpec((1, t, AUG), lambda i, j, w: (i, 0, 0)),
                  pl.BlockSpec((1, VROWS, t), lambda i, j, w: (i, 0, 0)),
                  pl.BlockSpec((1, t, AUG), lambda i, j, w: (i, 0, 0)),
                  pl.BlockSpec((1, VROWS, t), lambda i, j, w: (i, 0, 0))],
        out_specs=pl.BlockSpec((nqb * Q_BLOCK, NSA_GROUP * dh), lambda i, j, w: (j, i)),
        scratch_shapes=[pltpu.SMEM((nqb * nq,), jnp.int32),
                        pltpu.VMEM((nqb, 1, QL), F32), pltpu.VMEM((nqb, VROWS, QL), F32)],
    )
    return pl.pallas_call(
        functools.partial(_nsa_sw_kernel, nq=nq, nwords=nwords, nqb=nqb),
        grid_spec=grid_spec,
        out_shape=jax.ShapeDtypeStruct((t, g * NSA_GROUP * dh), BF16),
        compiler_params=_params(2),
        name="nsa_sw",
    )(words, qa, glog, oct_, selt, base, ks, vst, kw, vwt)


def kernel(x, a_norm_w, a_w_in, a_gnorm_w, a_w_out, a_lower_bounds, kv_norm_w, kv_w, cmp_pe_k, cmp_w1_k,
           cmp_w2_k, cmp_pe_v, cmp_w1_v, cmp_w2_v, b_norm_w, b_w_in, b_w_out, mlp_norm_w, mlp_w_up,
           mlp_w_down, final_norm_w):
    bsz, t, d = x.shape
    assert bsz == 1 and t % Q_BLOCK == 0 and t >= WINDOW + Q_BLOCK
    n_a = a_w_in.shape[0]
    n_b = b_w_in.shape[0]
    nq = t // Q_BLOCK
    xs = x[0]

    for layer in range(n_a):
        o = _hgrn(xs, a_norm_w[layer], a_w_in[layer].astype(BF16), a_lower_bounds, a_gnorm_w[layer], layer)
        xs = _proj_mlp(xs, o, a_w_out[layer].astype(BF16), mlp_norm_w[layer],
                       mlp_w_up[layer].astype(BF16), mlp_w_down[layer].astype(BF16),
                       final_norm_w, final=False)

    gd = NSA_KV_HEADS * NSA_DH
    kvw = kv_w.reshape(d, N_KV_STREAMS, NSA_KV_HEADS, NSA_DH)
    wc = kvw[:, 0:2].reshape(d, 2 * gd).astype(BF16)
    wk = jnp.pad(jnp.stack([kvw[:, 2], kvw[:, 4]], axis=1), ((0, 0), (0, 0), (0, 0), (0, AUG - NSA_DH)))
    wk = wk.reshape(d, 2 * NSA_KV_HEADS * AUG).astype(BF16)
    wvt = jnp.stack([kvw[:, 3], kvw[:, 5]], axis=1).reshape(d, 2 * gd).T.astype(BF16)
    kc_in, vc_in, ks, kw, vst, vwt = _kv_proj(xs, kv_norm_w, wc, wk, wvt)

    ncmp = t // CMP_STRIDE
    half = CMP_STRIDE * NSA_DH
    hf = jnp.stack([kc_in.reshape(NSA_KV_HEADS, ncmp, half), vc_in.reshape(NSA_KV_HEADS, ncmp, half)])
    pe = jnp.stack([cmp_pe_k.reshape(2, half), cmp_pe_v.reshape(2, half)])
    w1 = jnp.stack([cmp_w1_k, cmp_w1_v]).astype(BF16)
    w2 = jnp.stack([cmp_w2_k, cmp_w2_v]).astype(BF16)
    cmp_out = _compress(hf, pe, w1, w2)
    ratio = SLC_BLOCK // CMP_STRIDE
    ns = ncmp // ratio
    perm = cmp_out.reshape(2, NSA_KV_HEADS, ns, ratio, NSA_DH).transpose(0, 1, 3, 2, 4).reshape(
        2, NSA_KV_HEADS, ncmp, NSA_DH)
    crow = jnp.arange(ncmp)
    cmp_end = SLC_BLOCK * (crow % ns) + CMP_STRIDE * (crow // ns) + (CMP_BLOCK - 1)
    aug_cols = jnp.stack([cmp_end // KEY_TILE] * 3 + [jnp.ones_like(cmp_end)] * 3 + [cmp_end % KEY_TILE] * 3,
                         axis=1).astype(F32)
    aug_cols = jnp.pad(aug_cols, ((0, 0), (0, AUG - NSA_DH - aug_cols.shape[1])))
    kc = jnp.concatenate([perm[0], jnp.broadcast_to(aug_cols, (NSA_KV_HEADS, ncmp, AUG - NSA_DH))],
                         axis=2).astype(BF16)
    ones_rows = jnp.zeros((NSA_KV_HEADS, VROWS - NSA_DH, ncmp), F32).at[:, 0, :].set(1.0)
    vct = jnp.concatenate([perm[1].transpose(0, 2, 1), ones_rows], axis=1).astype(BF16)

    slopes = jnp.exp2(-8.0 * jnp.arange(1, NSA_HEADS + 1, dtype=F32) / NSA_HEADS)
    slopes = jnp.repeat(slopes.reshape(NSA_KV_HEADS, 1, NSA_GROUP), Q_BLOCK, axis=2)
    q_off = jnp.tile(jnp.arange(Q_BLOCK, dtype=F32), NSA_GROUP)[None, :]
    base = q_off - jnp.arange(KEY_TILE, dtype=F32)[:, None]
    cbase = q_off - float(SLC_BLOCK) * jnp.arange(ns, dtype=F32)[:, None]

    qw = NSA_HEADS * NSA_DH
    n_gate = b_w_in.shape[2] - qw
    pad_rows = (-(qw + n_gate)) % 128
    nwords = ns // WORD_BITS
    for b in range(n_b):
        w_q = b_w_in[b][:, 0:qw] * (NSA_DH ** -0.5)
        w_g = b_w_in[b][:, qw:].reshape(d, NSA_KV_HEADS, NSA_GROUP, 3).transpose(0, 1, 3, 2).reshape(d, n_gate)
        wt_in = jnp.pad(jnp.concatenate([w_q, w_g], axis=1), ((0, 0), (0, pad_rows))).T.astype(BF16)
        qa, glog = _nsa_inproj(xs, b_norm_w[b], wt_in, slopes)
        oct_, selt, words = _nsa_cmp(qa, kc, vct, cbase)
        words = words[:, :, 0, 0:nwords].reshape(-1)
        o = _nsa_sw(words, qa, glog, oct_, selt, base, ks, vst, kw, vwt)
        xs = _proj_mlp(xs, o, b_w_out[b].astype(BF16), mlp_norm_w[n_a + b],
                       mlp_w_up[n_a + b].astype(BF16), mlp_w_down[n_a + b].astype(BF16),
                       final_norm_w, final=(b == n_b - 1))
    return xs[None]
```

```python
import functools

import jax
import jax.numpy as jnp
from jax import lax
from jax.experimental import pallas as pl
from jax.experimental.pallas import tpu as pltpu

F32 = jnp.float32
BF16 = jnp.bfloat16

NORM_EPS = 1e-6
NEG_INF = -1e30
GATE_FLOOR = 1e-30

HG_HEADS = 8
HG_DK = 128
HG_DV = 128
HG_CHUNK = 64
HG_STEP_CHUNKS = 4

NSA_HEADS = 16
NSA_KV_HEADS = 4
NSA_GROUP = NSA_HEADS // NSA_KV_HEADS
NSA_DH = 64
CMP_BLOCK = 32
CMP_STRIDE = 16
SLC_BLOCK = 64
SLC_TOPK = 16
WINDOW = 512
Q_BLOCK = 128
FORCE_BONUS = 1e4
N_KV_STREAMS = 6

KEY_TILE = 128
SEL_STATIC = 8
SEL_CHUNK = 2
SW_QBLOCKS = 4
CMP_QBLOCKS = 4
WORD_BITS = 16
QL = NSA_GROUP * Q_BLOCK
AUG = 128
VROWS = NSA_DH + 16

VMEM_LIMIT = 56 * 1024 * 1024


def _params(n_axes, vmem=VMEM_LIMIT):
    return pltpu.CompilerParams(dimension_semantics=("arbitrary",) * n_axes, vmem_limit_bytes=vmem)


def _nt(a, b):
    return lax.dot_general(a, b, (((1,), (1,)), ((), ())), preferred_element_type=F32)


def _tn(a, b):
    return lax.dot_general(a, b, (((0,), (0,)), ((), ())), preferred_element_type=F32)


def _dot(a, b):
    return jnp.dot(a, b, preferred_element_type=F32)


def _rmsnorm(x, w):
    ms = jnp.mean(x * x, axis=-1, keepdims=True)
    return x * lax.rsqrt(ms + NORM_EPS) * w


def _hgrn_kernel(x_ref, nw_ref, w_ref, lbp_ref, gw_ref, seg_ref, o_ref, st_ref, *, layer, n_chunks):
    c = HG_CHUNK
    hk = HG_HEADS * HG_DK

    @pl.when(pl.program_id(0) == 0)
    def _():
        st_ref[...] = jnp.zeros_like(st_ref)

    hn = _rmsnorm(x_ref[...], nw_ref[...]).astype(BF16)
    proj = {grp: _dot(hn, w_ref[:, grp * hk:(grp + 1) * hk]) for grp in (1, 0, 2, 3)}

    a = lbp_ref[...]
    e = jnp.exp(a - jnp.max(a, axis=0, keepdims=True))
    p = e / jnp.sum(e, axis=0, keepdims=True)
    cum = p[0:1]
    for i in range(1, layer + 1):
        cum = cum + p[i:i + 1]
    lb = cum - p[0:1]
    one_m = 1.0 - lb

    sub = lax.broadcasted_iota(jnp.int32, (8, hk), 0)
    ti = lax.broadcasted_iota(jnp.int32, (c, c), 0)
    si = lax.broadcasted_iota(jnp.int32, (c, c), 1)
    masks = [(((ti >> j) & 1) == 1) & (((si >> j) & 1) == 0) & ((ti >> (j + 1)) == (si >> (j + 1)))
             for j in range(6)]
    diag = ti == si
    gw = gw_ref[...]

    for ch in range(n_chunks):
        rows = slice(ch * c, (ch + 1) * c)
        q = proj[0][rows]
        fp = proj[1][rows]
        v = proj[2][rows]
        go = proj[3][rows]

        ea = jnp.exp(-jnp.abs(fp))
        r = 1.0 / (1.0 + ea)
        pos = fp >= 0
        f_gate = lb + one_m * jnp.where(pos, r, ea * r)
        kk = one_m * jnp.where(pos, ea * r, r)
        logf = jnp.log(jnp.maximum(f_gate, GATE_FLOOR))

        hi = logf.astype(BF16)
        rem = logf - hi.astype(F32)
        mid = rem.astype(BF16)
        lo = (rem - mid.astype(F32)).astype(BF16)
        sums = _dot(seg_ref[...], jnp.concatenate([hi, mid, lo], axis=0))
        b = sums[0:c]
        eb = jnp.exp(b[c - 1:c, :])

        xs = []
        for j in range(6):
            ex = jnp.exp(sums[c * (j + 1):c * (j + 2)])
            slabs = []
            for r0 in range(0, c, 8):
                if j < 3:
                    slabs.append(jnp.where(((sub >> j) & 1) == 1, q[r0:r0 + 8], kk[r0:r0 + 8]))
                else:
                    slabs.append(q[r0:r0 + 8] if (r0 >> j) & 1 else kk[r0:r0 + 8])
            xs.append((jnp.concatenate(slabs, axis=0) * ex).astype(BF16))

        qb = q.astype(BF16)
        kb = kk.astype(BF16)
        vb = v.astype(BF16)
        qs = (q * jnp.exp(b)).astype(BF16)
        khat = (kk * jnp.exp(sums[7 * c:8 * c])).astype(BF16)

        heads = [slice(h * HG_DK, (h + 1) * HG_DK) for h in range(HG_HEADS)]
        attn = [jnp.where(diag, _nt(qb[:, sl], kb[:, sl]), 0.0) for sl in heads]
        for j in range(6):
            for h, sl in enumerate(heads):
                xj = xs[j][:, sl]
                attn[h] = attn[h] + jnp.where(masks[j], _nt(xj, xj), 0.0)
        outs = []
        for h, sl in enumerate(heads):
            st = st_ref[h]
            outs.append(_dot(attn[h].astype(BF16), vb[:, sl]) + _nt(qs[:, sl], st.astype(BF16)))
            st_ref[h] = st * eb[:, sl] + _tn(vb[:, sl], khat[:, sl])
        for h, sl in enumerate(heads):
            g = go[:, sl]
            o_ref[rows, sl] = (_rmsnorm(outs[h], gw) * (g * (1.0 / (1.0 + jnp.exp(-g))))).astype(o_ref.dtype)


def _segment_matrix():
    c = HG_CHUNK
    t = jnp.arange(c)[:, None]
    i = jnp.arange(c)[None, :]
    parts = [i <= t]
    for j in range(6):
        m = 1 << j
        ref = t - (t % (2 * m)) + m
        parts.append(jnp.where(t >= ref, (i > ref) & (i <= t), (i > t) & (i <= ref)))
    parts.append(i > t)
    return jnp.tile(jnp.concatenate(parts, axis=0), (1, 3)).astype(BF16)


def _hgrn(x, norm_w, w_in, lb_param, gnorm_w, layer):
    t, d = x.shape
    hk = HG_HEADS * HG_DK
    hv = HG_HEADS * HG_DV
    n_layers = lb_param.shape[0]
    seg = _segment_matrix()
    tm = HG_STEP_CHUNKS * HG_CHUNK
    const = lambda i: (0, 0)
    return pl.pallas_call(
        functools.partial(_hgrn_kernel, layer=layer, n_chunks=HG_STEP_CHUNKS),
        grid=(t // tm,),
        in_specs=[pl.BlockSpec((tm, d), lambda i: (i, 0)),
                  pl.BlockSpec((1, d), const),
                  pl.BlockSpec(w_in.shape, const),
                  pl.BlockSpec((n_layers, hk), const),
                  pl.BlockSpec((1, HG_DV), const),
                  pl.BlockSpec(seg.shape, const)],
        out_specs=pl.BlockSpec((tm, hv), lambda i: (i, 0)),
        out_shape=jax.ShapeDtypeStruct((t, hv), BF16),
        scratch_shapes=[pltpu.VMEM((HG_HEADS, HG_DV, HG_DK), F32)],
        compiler_params=_params(1),
        name="hgrn",
    )(x, norm_w.reshape(1, d), w_in, lb_param, gnorm_w.reshape(1, HG_DV), seg)


def _proj_mlp_kernel(x_ref, a_ref, wo_ref, nw_ref, wup_ref, wdn_ref, fnw_ref, o_ref, *, final, ff_tile):
    x1 = x_ref[...] + _dot(a_ref[...], wo_ref[...])
    h = _rmsnorm(x1, nw_ref[...]).astype(BF16)
    acc = x1
    d_ff = wup_ref.shape[1]
    for c0 in range(0, d_ff, ff_tile):
        u = jnp.maximum(_dot(h, wup_ref[:, c0:c0 + ff_tile]), 0.0)
        acc = acc + _dot((u * u).astype(BF16), wdn_ref[c0:c0 + ff_tile, :])
    if final:
        acc = _rmsnorm(acc, fnw_ref[...])
    o_ref[...] = acc


def _proj_mlp(x, a, wo, nw, wup, wdn, fnw, final, tm=512):
    t, d = x.shape
    d_ff = wup.shape[1]
    const = lambda i: (0, 0)
    return pl.pallas_call(
        functools.partial(_proj_mlp_kernel, final=final, ff_tile=min(1024, d_ff)),
        grid=(t // tm,),
        in_specs=[pl.BlockSpec((tm, d), lambda i: (i, 0)),
                  pl.BlockSpec((tm, a.shape[1]), lambda i: (i, 0)),
                  pl.BlockSpec(wo.shape, const),
                  pl.BlockSpec((1, d), const),
                  pl.BlockSpec(wup.shape, const),
                  pl.BlockSpec(wdn.shape, const),
                  pl.BlockSpec((1, d), const)],
        out_specs=pl.BlockSpec((tm, d), lambda i: (i, 0)),
        out_shape=jax.ShapeDtypeStruct((t, d), F32),
        compiler_params=_params(1),
        name="proj_mlp",
    )(x, a, wo, nw.reshape(1, d), wup, wdn, fnw.reshape(1, d))


def _kv_proj_kernel(x_ref, nw_ref, wc_ref, wk_ref, wvt_ref, kc_ref, vc_ref, ks_ref, kw_ref, vst_ref, vwt_ref):
    i = pl.program_id(0)
    tm = x_ref.shape[0]
    gd = NSA_KV_HEADS * NSA_DH
    h = _rmsnorm(x_ref[...], nw_ref[...]).astype(BF16)
    cmp_in = _dot(h, wc_ref[...])
    for g in range(NSA_KV_HEADS):
        kc_ref[g] = cmp_in[:, g * NSA_DH:(g + 1) * NSA_DH]
        vc_ref[g] = cmp_in[:, gd + g * NSA_DH:gd + (g + 1) * NSA_DH]
    kk = _dot(h, wk_ref[...])
    row = lax.broadcasted_iota(jnp.int32, (tm, AUG), 0)
    lane = lax.broadcasted_iota(jnp.int32, (tm, AUG), 1)
    tok = i * tm + row
    tile = (tok // KEY_TILE).astype(F32)
    offs = (tok % KEY_TILE).astype(F32)
    consts = jnp.where((lane >= NSA_DH) & (lane < NSA_DH + 3), tile,
                       jnp.where((lane >= NSA_DH + 3) & (lane < NSA_DH + 6), 1.0,
                                 jnp.where((lane >= NSA_DH + 6) & (lane < NSA_DH + 9), offs, 0.0)))
    for g in range(NSA_KV_HEADS):
        ks_ref[g] = (kk[:, g * AUG:(g + 1) * AUG] + consts).astype(BF16)
        kw_ref[g] = (kk[:, (NSA_KV_HEADS + g) * AUG:(NSA_KV_HEADS + g + 1) * AUG] + consts).astype(BF16)
    vt = _nt(wvt_ref[...], h)
    r16 = lax.broadcasted_iota(jnp.int32, (VROWS - NSA_DH, tm), 0)
    ones_blk = jnp.where(r16 == 0, 1.0, 0.0).astype(BF16)
    for g in range(NSA_KV_HEADS):
        vst_ref[g, 0:NSA_DH, :] = vt[g * NSA_DH:(g + 1) * NSA_DH, :].astype(BF16)
        vst_ref[g, NSA_DH:VROWS, :] = ones_blk
        vwt_ref[g, 0:NSA_DH, :] = vt[gd + g * NSA_DH:gd + (g + 1) * NSA_DH, :].astype(BF16)
        vwt_ref[g, NSA_DH:VROWS, :] = ones_blk


def _kv_proj(x, nw, wc, wk, wvt, tm=256):
    t, d = x.shape
    g = NSA_KV_HEADS
    const = lambda i: (0, 0)
    rows = lambda i: (0, i, 0)
    cols = lambda i: (0, 0, i)
    return pl.pallas_call(
        _kv_proj_kernel,
        grid=(t // tm,),
        in_specs=[pl.BlockSpec((tm, d), lambda i: (i, 0)),
                  pl.BlockSpec((1, d), const),
                  pl.BlockSpec(wc.shape, const),
                  pl.BlockSpec(wk.shape, const),
                  pl.BlockSpec(wvt.shape, const)],
        out_specs=[pl.BlockSpec((g, tm, NSA_DH), rows), pl.BlockSpec((g, tm, NSA_DH), rows),
                   pl.BlockSpec((g, tm, AUG), rows), pl.BlockSpec((g, tm, AUG), rows),
                   pl.BlockSpec((g, VROWS, tm), cols), pl.BlockSpec((g, VROWS, tm), cols)],
        out_shape=[jax.ShapeDtypeStruct((g, t, NSA_DH), F32), jax.ShapeDtypeStruct((g, t, NSA_DH), F32),
                   jax.ShapeDtypeStruct((g, t, AUG), BF16), jax.ShapeDtypeStruct((g, t, AUG), BF16),
                   jax.ShapeDtypeStruct((g, VROWS, t), BF16), jax.ShapeDtypeStruct((g, VROWS, t), BF16)],
        compiler_params=_params(1),
        name="kv_proj",
    )(x, nw.reshape(1, d), wc, wk, wvt)


def _nsa_inproj_kernel(x_ref, nw_ref, wt_ref, sl_ref, q_ref, gl_ref, *, nqb):
    i = pl.program_id(0)
    qw = NSA_HEADS * NSA_DH
    h = _rmsnorm(x_ref[...], nw_ref[...]).astype(BF16)
    pt = _nt(wt_ref[...], h)
    r16 = lax.broadcasted_iota(jnp.int32, (16, QL), 0)
    ql = (lax.broadcasted_iota(jnp.int32, (1, QL), 1) & (Q_BLOCK - 1)).astype(F32)

    def split3(v):
        a = v.astype(BF16).astype(F32)
        b = (v - a).astype(BF16).astype(F32)
        return a, b, v - a - b

    for g in range(NSA_KV_HEADS):
        slope = sl_ref[g]
        s1, s2, s3 = split3(slope * float(KEY_TILE))
        r1, r2, r3 = split3(slope)
        for qb in range(nqb):
            lanes = slice(qb * Q_BLOCK, (qb + 1) * Q_BLOCK)
            for j in range(NSA_GROUP):
                r0 = (g * NSA_GROUP + j) * NSA_DH
                q_ref[g, qb, 0:NSA_DH, j * Q_BLOCK:(j + 1) * Q_BLOCK] = pt[r0:r0 + NSA_DH, lanes].astype(BF16)
            tq = ((i * nqb + qb) * Q_BLOCK).astype(F32) + ql
            o1, o2, o3 = split3(-(slope * tq))
            aug = jnp.zeros((16, QL), F32)
            for k, term in enumerate((s1, s2, s3, o1, o2, o3, r1, r2, r3)):
                aug = jnp.where(r16 == k, term, aug)
            q_ref[g, qb, NSA_DH:NSA_DH + 16, :] = aug.astype(BF16)
            q_ref[g, qb, NSA_DH + 16:AUG, :] = jnp.zeros((AUG - NSA_DH - 16, QL), BF16)
            for c in range(3):
                for j in range(NSA_GROUP):
                    r0 = qw + (g * 3 + c) * NSA_GROUP + j
                    gl_ref[g, qb, c:c + 1, j * Q_BLOCK:(j + 1) * Q_BLOCK] = pt[r0:r0 + 1, lanes]


def _nsa_inproj(x, nw, wt, slopes, nqb=2):
    t, d = x.shape
    nq = t // Q_BLOCK
    g = NSA_KV_HEADS
    tm = nqb * Q_BLOCK
    return pl.pallas_call(
        functools.partial(_nsa_inproj_kernel, nqb=nqb),
        grid=(t // tm,),
        in_specs=[pl.BlockSpec((tm, d), lambda i: (i, 0)),
                  pl.BlockSpec((1, d), lambda i: (0, 0)),
                  pl.BlockSpec(wt.shape, lambda i: (0, 0)),
                  pl.BlockSpec(slopes.shape, lambda i: (0, 0, 0))],
        out_specs=[pl.BlockSpec((g, nqb, AUG, QL), lambda i: (0, i, 0, 0)),
                   pl.BlockSpec((g, nqb, 3, QL), lambda i: (0, i, 0, 0))],
        out_shape=[jax.ShapeDtypeStruct((g, nq, AUG, QL), BF16),
                   jax.ShapeDtypeStruct((g, nq, 3, QL), F32)],
        compiler_params=_params(1),
        name="nsa_inproj",
    )(x, nw.reshape(1, d), wt, slopes)


def _compress_kernel(hf_ref, pe_ref, w1_ref, w2_ref, o_ref):
    x = hf_ref[0, 0]
    nc, half = x.shape
    pe_lo = pe_ref[0, 0:1, :]
    pe_hi = pe_ref[0, 1:2, :]
    w1_lo = w1_ref[0, 0:half, :]
    w1_hi = w1_ref[0, half:2 * half, :]
    u = _dot((x + pe_lo).astype(BF16), w1_lo)
    vv = _dot((x + pe_hi).astype(BF16), w1_hi)
    nxt = pltpu.roll(vv, nc - 1, 0)
    pad = _dot(jnp.broadcast_to(pe_hi, (8, half)).astype(BF16), w1_hi)[0:1]
    row = lax.broadcasted_iota(jnp.int32, vv.shape, 0)
    pre = u + jnp.where(row == nc - 1, pad, nxt)
    hid = 0.5 * pre * (1.0 + jnp.tanh(0.7978845608028654 * (pre + 0.044715 * (pre * pre * pre))))
    o_ref[0, 0] = _dot(hid.astype(BF16), w2_ref[0])


def _compress(hf, pe, w1, w2):
    s, g, nc, half = hf.shape
    return pl.pallas_call(
        _compress_kernel,
        grid=(s, g),
        in_specs=[pl.BlockSpec((1, 1, nc, half), lambda i, j: (i, j, 0, 0)),
                  pl.BlockSpec((1, 2, half), lambda i, j: (i, 0, 0)),
                  pl.BlockSpec((1,) + w1.shape[1:], lambda i, j: (i, 0, 0)),
                  pl.BlockSpec((1,) + w2.shape[1:], lambda i, j: (i, 0, 0))],
        out_specs=pl.BlockSpec((1, 1, nc, w2.shape[2]), lambda i, j: (i, j, 0, 0)),
        out_shape=jax.ShapeDtypeStruct((s, g, nc, w2.shape[2]), F32),
        compiler_params=_params(2),
        name="compress",
    )(hf, pe, w1, w2)


def _nsa_cmp_kernel(q_ref, kc_ref, vct_ref, cb_ref, oc_ref, sel_ref, w_ref, p_ref, *, ns, topk, nqb):
    ratio = SLC_BLOCK // CMP_STRIDE
    qis = [pl.program_id(1) * nqb + qb for qb in range(nqb)]

    def attend(nv):
        cb = cb_ref[0:nv, :]
        n_i = lax.broadcasted_iota(jnp.int32, (nv, Q_BLOCK), 0)
        ql = lax.broadcasted_iota(jnp.int32, (nv, Q_BLOCK), 1)
        n_f = n_i.astype(F32)
        mxs = []
        for qb in range(nqb):
            qa = q_ref[0, qb]
            t0f = (qis[qb] * Q_BLOCK).astype(F32)
            mx = jnp.full((1, QL), NEG_INF, F32)
            for u in range(ratio):
                s = _dot(kc_ref[0, u * ns:u * ns + nv, :], qa)
                s = jnp.where(cb >= float(CMP_STRIDE * u + CMP_BLOCK - 1) - t0f, s, NEG_INF)
                p_ref[qb, u * ns:u * ns + nv, :] = s
                mx = jnp.maximum(mx, jnp.max(s, axis=0, keepdims=True))
            mxs.append(mx)
        cands, forceds = [], []
        for qb in range(nqb):
            mx = mxs[qb]
            valid = mx > 0.5 * NEG_INF
            acc = jnp.zeros((VROWS, QL), F32)
            head_pool = [None] * NSA_GROUP
            for u in range(ratio):
                p = jnp.exp(p_ref[qb, u * ns:u * ns + nv, :] - mx)
                acc = acc + _dot(vct_ref[0, :, u * ns:u * ns + nv], p.astype(BF16))
                for j in range(NSA_GROUP):
                    pj = p[:, j * Q_BLOCK:(j + 1) * Q_BLOCK]
                    head_pool[j] = pj if head_pool[j] is None else head_pool[j] + pj
                    if u == ratio - 1:
                        head_pool[j] = head_pool[j] + jnp.where(n_i >= 1, pltpu.roll(pj, 1, 0), 0.0)
            inv = jnp.where(valid, 1.0 / acc[NSA_DH:NSA_DH + 1], 0.0)
            oc_ref[0, qb] = acc[0:NSA_DH] * inv
            pooled = head_pool[0] * inv[:, 0:Q_BLOCK]
            for j in range(1, NSA_GROUP):
                pooled = pooled + head_pool[j] * inv[:, j * Q_BLOCK:(j + 1) * Q_BLOCK]
            cur = 2 * qis[qb] + (ql >= SLC_BLOCK).astype(jnp.int32)
            ok = n_i <= cur
            forced = ok & ((n_i == 0) | (n_i == cur) | (n_i == cur - 1))
            forceds.append(forced)
            cands.append(jnp.where(ok & jnp.logical_not(forced), pooled, NEG_INF))
        scores = list(cands)
        for _ in range(topk - 3):
            for qb in range(nqb):
                best = jnp.max(scores[qb], axis=0, keepdims=True)
                idx = jnp.min(jnp.where(scores[qb] == best, n_f, float(ns)), axis=0, keepdims=True)
                scores[qb] = jnp.where(n_f == idx, -jnp.inf, scores[qb])
        wk = lax.broadcasted_iota(jnp.int32, (nv, 128), 1)
        wn = lax.broadcasted_iota(jnp.int32, (nv, 128), 0)
        weight = jnp.where((wn // WORD_BITS) == wk,
                           jnp.left_shift(1, wn & (WORD_BITS - 1)).astype(F32), 0.0)
        for qb in range(nqb):
            picked = (scores[qb] == -jnp.inf) & (cands[qb] > 0.5 * NEG_INF)
            sel = jnp.where(forceds[qb] | picked, 1.0, 0.0)
            sel_ref[0, qb, 0:nv, :] = sel
            if nv < ns:
                sel_ref[0, qb, nv:ns, :] = jnp.zeros((ns - nv, Q_BLOCK), F32)
            any_q = jnp.max(sel, axis=1, keepdims=True)
            w_ref[0, qb] = jnp.sum(any_q * weight, axis=0, keepdims=True).astype(jnp.int32)

    n_cls = 8
    step = ns // n_cls
    need = 2 * qis[-1] + 2
    for k in range(n_cls):
        lo, hi = k * step, (k + 1) * step

        @pl.when((need > lo) & (need <= hi))
        def _():
            attend(hi)


def _nsa_cmp(qa, kc, vct, cbase):
    g, nq, _, _ = qa.shape
    dh = NSA_DH
    ncmp = kc.shape[1]
    ns = ncmp // (SLC_BLOCK // CMP_STRIDE)
    topk = min(SLC_TOPK, ns)
    nqb = CMP_QBLOCKS
    return pl.pallas_call(
        functools.partial(_nsa_cmp_kernel, ns=ns, topk=topk, nqb=nqb),
        grid=(g, nq // nqb),
        in_specs=[pl.BlockSpec((1, nqb, AUG, QL), lambda i, j: (i, j, 0, 0)),
                  pl.BlockSpec((1, ncmp, AUG), lambda i, j: (i, 0, 0)),
                  pl.BlockSpec((1, VROWS, ncmp), lambda i, j: (i, 0, 0)),
                  pl.BlockSpec((ns, QL), lambda i, j: (0, 0))],
        out_specs=[pl.BlockSpec((1, nqb, dh, QL), lambda i, j: (i, j, 0, 0)),
                   pl.BlockSpec((1, nqb, ns, Q_BLOCK), lambda i, j: (i, j, 0, 0)),
                   pl.BlockSpec((1, nqb, 1, 128), lambda i, j: (i, j, 0, 0))],
        out_shape=[jax.ShapeDtypeStruct((g, nq, dh, QL), F32),
                   jax.ShapeDtypeStruct((g, nq, ns, Q_BLOCK), F32),
                   jax.ShapeDtypeStruct((g, nq, 1, 128), jnp.int32)],
        scratch_shapes=[pltpu.VMEM((nqb, ncmp, QL), F32)],
        compiler_params=_params(2),
        name="nsa_cmp",
    )(qa, kc, vct, cbase)


def _nsa_sw_kernel(words_ref, q_ref, gl_ref, oc_ref, sel_ref, base_ref, ks_ref, vst_ref,
                   kw_ref, vwt_ref, o_ref, list_ref, m_ref, acc_ref, *, nq, nwords, nqb):
    g = pl.program_id(0)
    step = pl.program_id(1)
    half = SLC_BLOCK
    tile_base = base_ref[...]
    per_word = WORD_BITS // 2

    def list_tiles(qb, qi):
        off = qb * nq
        for u in range(SEL_STATIC):
            list_ref[off + u] = 0
        wbase = (g * nq + qi) * nwords

        def word_body(wi, cnt):
            word = words_ref[wbase + wi]

            def scan(cnt):
                for bp in range(per_word):
                    m = wi * per_word + bp
                    bits = lax.shift_right_logical(word, 2 * bp) & 3
                    list_ref[off + cnt] = m
                    cnt = cnt + jnp.logical_and(bits != 0, m < qi).astype(jnp.int32)
                return cnt

            return lax.cond(word != 0, scan, lambda c: c, cnt)

        cnt = lax.fori_loop(0, (qi + per_word - 1) // per_word, word_body, jnp.int32(0))
        for k in range(1, SEL_CHUNK):
            list_ref[off + jnp.minimum(cnt + k, nq - 1)] = 0
        return cnt

    def probs(s, mx):
        return jnp.exp((s - mx).astype(BF16))

    def sel_mask(ra, rb):
        mk = jnp.concatenate([jnp.broadcast_to(ra, (half, Q_BLOCK)), jnp.broadcast_to(rb, (half, Q_BLOCK))], axis=0)
        return jnp.concatenate([mk] * NSA_GROUP, axis=1) > 0.5

    def listed_tile(qb, cnt, slot):
        live = slot < cnt
        m = jnp.where(live, list_ref[qb * nq + jnp.minimum(slot, nq - 1)], 0)
        keep = jnp.where(live, 1.0, 0.0)
        pos = pl.multiple_of(m * KEY_TILE, KEY_TILE)
        mask = sel_mask(sel_ref[0, qb, pl.ds(2 * m, 1), :] * keep, sel_ref[0, qb, pl.ds(2 * m + 1, 1), :] * keep)
        return ks_ref[0, pl.ds(pos, KEY_TILE), :], vst_ref[0, :, pl.ds(pos, KEY_TILE)], mask

    def masked_scores(qa, k_tiles, masks):
        sc = _dot(jnp.concatenate(k_tiles, axis=0), qa)
        tiles, mx = [], None
        for u, mask in enumerate(masks):
            s = jnp.where(mask, sc[u * KEY_TILE:(u + 1) * KEY_TILE, :], NEG_INF)
            mu = jnp.max(s, axis=0, keepdims=True)
            mx = mu if mx is None else jnp.maximum(mx, mu)
            tiles.append(s)
        return tiles, mx

    def pv(s_tiles, v_tiles, mx):
        return _dot(jnp.concatenate(v_tiles, axis=1), jnp.concatenate([probs(s, mx) for s in s_tiles], axis=0))

    qis = [step * nqb + qb for qb in range(nqb)]
    cnts = [list_tiles(qb, qis[qb]) for qb in range(nqb)]

    n_wt = (WINDOW + Q_BLOCK) // KEY_TILE
    staged = []
    for qb in range(nqb):
        qi, cnt = qis[qb], cnts[qb]
        t0 = qi * Q_BLOCK
        qa = q_ref[0, qb]
        wk, wv, wmask = [], [], []
        for u in range(n_wt):
            pos_raw = t0 - WINDOW + u * KEY_TILE
            pos = pl.multiple_of(jnp.maximum(pos_raw, 0), KEY_TILE)
            wk.append(kw_ref[0, pl.ds(pos, KEY_TILE), :])
            wv.append(vwt_ref[0, :, pl.ds(pos, KEY_TILE)])
            live = pos_raw >= 0
            if u == 0:
                wmask.append(jnp.logical_and(tile_base < 0, live))
            elif u == n_wt - 1:
                wmask.append(tile_base >= 0)
            else:
                wmask.append(live)
        posd = pl.multiple_of(t0, KEY_TILE)
        diag_mask = sel_mask(sel_ref[0, qb, pl.ds(2 * qi, 1), :],
                             sel_ref[0, qb, pl.ds(2 * qi + 1, 1), :]) & (tile_base >= 0)
        listed = [listed_tile(qb, cnt, jnp.int32(u)) for u in range(SEL_STATIC)]
        stiles, m0 = masked_scores(qa, [ks_ref[0, pl.ds(posd, KEY_TILE), :]] + [t[0] for t in listed],
                                   [diag_mask] + [t[2] for t in listed])
        wtiles, mw = masked_scores(qa, wk, wmask)
        staged.append((stiles, m0, [vst_ref[0, :, pl.ds(posd, KEY_TILE)]] + [t[1] for t in listed], wtiles, mw, wv))

    o_wins = []
    for qb in range(nqb):
        stiles, m0, svals, wtiles, mw, wv = staged[qb]
        m_ref[qb] = m0
        acc_ref[qb] = pv(stiles, svals, m0)
        accw = pv(wtiles, wv, mw)
        o_wins.append(accw[0:NSA_DH] * (1.0 / accw[NSA_DH:NSA_DH + 1]))

    for qb in range(nqb):
        cnt = cnts[qb]
        qa = q_ref[0, qb]

        def chunk_body(ci, carry, qb=qb, cnt=cnt, qa=qa):
            listed = [listed_tile(qb, cnt, SEL_STATIC + ci * SEL_CHUNK + u) for u in range(SEL_CHUNK)]
            tiles, cmax = masked_scores(qa, [t[0] for t in listed], [t[2] for t in listed])
            m_old = m_ref[qb]
            m_new = jnp.maximum(m_old, cmax)
            acc_ref[qb] = jnp.exp(m_old - m_new) * acc_ref[qb] + pv(tiles, [t[1] for t in listed], m_new)
            m_ref[qb] = m_new
            return carry

        lax.fori_loop(0, (jnp.maximum(cnt - SEL_STATIC, 0) + SEL_CHUNK - 1) // SEL_CHUNK, chunk_body, 0)
        o_sel = acc_ref[qb, 0:NSA_DH, :] * (1.0 / acc_ref[qb, NSA_DH:NSA_DH + 1, :])

        gl = gl_ref[0, qb]
        gate = 1.0 / (1.0 + jnp.exp(-gl))
        out_t = gate[0:1] * oc_ref[0, qb] + gate[1:2] * o_sel + gate[2:3] * o_wins[qb]
        halves = []
        for j in range(0, NSA_GROUP, 2):
            pair = jnp.concatenate([out_t[:, j * Q_BLOCK:(j + 1) * Q_BLOCK],
                                    out_t[:, (j + 1) * Q_BLOCK:(j + 2) * Q_BLOCK]], axis=0)
            halves.append(pair.T)
        o_ref[qb * Q_BLOCK:(qb + 1) * Q_BLOCK, :] = jnp.concatenate(halves, axis=1).astype(o_ref.dtype)


def _nsa_sw(words, qa, glog, oct_, selt, base, ks, vst, kw, vwt):
    g, nq, _, _ = qa.shape
    dh = NSA_DH
    t = ks.shape[1]
    ns = selt.shape[2]
    nwords = words.shape[0] // (g * nq)
    nqb = SW_QBLOCKS
    grid_spec = pltpu.PrefetchScalarGridSpec(
        num_scalar_prefetch=1,
        grid=(g, nq // nqb),
        in_specs=[pl.BlockSpec((1, nqb, AUG, QL), lambda i, j, w: (i, j, 0, 0)),
                  pl.BlockSpec((1, nqb, 3, QL), lambda i, j, w: (i, j, 0, 0)),
                  pl.BlockSpec((1, nqb, dh, QL), lambda i, j, w: (i, j, 0, 0)),
                  pl.BlockSpec((1, nqb, ns, Q_BLOCK), lambda i, j, w: (i, j, 0, 0)),
                  pl.BlockSpec((KEY_TILE, QL), lambda i, j, w: (0, 0)),
                  pl.BlockSpec((1, t, AUG), lambda i, j, w: (i, 0, 0)),
                  pl.BlockSpec((1, VROWS, t), lambda i, j, w: (i, 0, 0)),
                  pl.BlockSpec((1, t, AUG), lambda i, j, w: (i, 0, 0)),
                  pl.BlockSpec((1, VROWS, t), lambda i, j, w: (i, 0, 0))],
        out_specs=pl.BlockSpec((nqb * Q_BLOCK, NSA_GROUP * dh), lambda i, j, w: (j, i)),
        scratch_shapes=[pltpu.SMEM((nqb * nq,), jnp.int32),
                        pltpu.VMEM((nqb, 1, QL), F32), pltpu.VMEM((nqb, VROWS, QL), F32)],
    )
    return pl.pallas_call(
        functools.partial(_nsa_sw_kernel, nq=nq, nwords=nwords, nqb=nqb),
        grid_spec=grid_spec,
        out_shape=jax.ShapeDtypeStruct((t, g * NSA_GROUP * dh), BF16),
        compiler_params=_params(2),
        name="nsa_sw",
    )(words, qa, glog, oct_, selt, base, ks, vst, kw, vwt)


def kernel(x, a_norm_w, a_w_in, a_gnorm_w, a_w_out, a_lower_bounds, kv_norm_w, kv_w, cmp_pe_k, cmp_w1_k,
           cmp_w2_k, cmp_pe_v, cmp_w1_v, cmp_w2_v, b_norm_w, b_w_in, b_w_out, mlp_norm_w, mlp_w_up,
           mlp_w_down, final_norm_w):
    bsz, t, d = x.shape
    assert bsz == 1 and t % Q_BLOCK == 0 and t >= WINDOW + Q_BLOCK
    n_a = a_w_in.shape[0]
    n_b = b_w_in.shape[0]
    nq = t // Q_BLOCK
    xs = x[0]

    for layer in range(n_a):
        o = _hgrn(xs, a_norm_w[layer], a_w_in[layer].astype(BF16), a_lower_bounds, a_gnorm_w[layer], layer)
        xs = _proj_mlp(xs, o, a_w_out[layer].astype(BF16), mlp_norm_w[layer],
                       mlp_w_up[layer].astype(BF16), mlp_w_down[layer].astype(BF16),
                       final_norm_w, final=False)

    gd = NSA_KV_HEADS * NSA_DH
    kvw = kv_w.reshape(d, N_KV_STREAMS, NSA_KV_HEADS, NSA_DH)
    wc = kvw[:, 0:2].reshape(d, 2 * gd).astype(BF16)
    wk = jnp.pad(jnp.stack([kvw[:, 2], kvw[:, 4]], axis=1), ((0, 0), (0, 0), (0, 0), (0, AUG - NSA_DH)))
    wk = wk.reshape(d, 2 * NSA_KV_HEADS * AUG).astype(BF16)
    wvt = jnp.stack([kvw[:, 3], kvw[:, 5]], axis=1).reshape(d, 2 * gd).T.astype(BF16)
    kc_in, vc_in, ks, kw, vst, vwt = _kv_proj(xs, kv_norm_w, wc, wk, wvt)

    ncmp = t // CMP_STRIDE
    half = CMP_STRIDE * NSA_DH
    hf = jnp.stack([kc_in.reshape(NSA_KV_HEADS, ncmp, half), vc_in.reshape(NSA_KV_HEADS, ncmp, half)])
    pe = jnp.stack([cmp_pe_k.reshape(2, half), cmp_pe_v.reshape(2, half)])
    w1 = jnp.stack([cmp_w1_k, cmp_w1_v]).astype(BF16)
    w2 = jnp.stack([cmp_w2_k, cmp_w2_v]).astype(BF16)
    cmp_out = _compress(hf, pe, w1, w2)
    ratio = SLC_BLOCK // CMP_STRIDE
    ns = ncmp // ratio
    perm = cmp_out.reshape(2, NSA_KV_HEADS, ns, ratio, NSA_DH).transpose(0, 1, 3, 2, 4).reshape(
        2, NSA_KV_HEADS, ncmp, NSA_DH)
    crow = jnp.arange(ncmp)
    cmp_end = SLC_BLOCK * (crow % ns) + CMP_STRIDE * (crow // ns) + (CMP_BLOCK - 1)
    aug_cols = jnp.stack([cmp_end // KEY_TILE] * 3 + [jnp.ones_like(cmp_end)] * 3 + [cmp_end % KEY_TILE] * 3,
                         axis=1).astype(F32)
    aug_cols = jnp.pad(aug_cols, ((0, 0), (0, AUG - NSA_DH - aug_cols.shape[1])))
    kc = jnp.concatenate([perm[0], jnp.broadcast_to(aug_cols, (NSA_KV_HEADS, ncmp, AUG - NSA_DH))],
                         axis=2).astype(BF16)
    ones_rows = jnp.zeros((NSA_KV_HEADS, VROWS - NSA_DH, ncmp), F32).at[:, 0, :].set(1.0)
    vct = jnp.concatenate([perm[1].transpose(0, 2, 1), ones_rows], axis=1).astype(BF16)

    slopes = jnp.exp2(-8.0 * jnp.arange(1, NSA_HEADS + 1, dtype=F32) / NSA_HEADS)
    slopes = jnp.repeat(slopes.reshape(NSA_KV_HEADS, 1, NSA_GROUP), Q_BLOCK, axis=2)
    q_off = jnp.tile(jnp.arange(Q_BLOCK, dtype=F32), NSA_GROUP)[None, :]
    base = q_off - jnp.arange(KEY_TILE, dtype=F32)[:, None]
    cbase = q_off - float(SLC_BLOCK) * jnp.arange(ns, dtype=F32)[:, None]

    qw = NSA_HEADS * NSA_DH
    n_gate = b_w_in.shape[2] - qw
    pad_rows = (-(qw + n_gate)) % 128
    nwords = ns // WORD_BITS
    for b in range(n_b):
        w_q = b_w_in[b][:, 0:qw] * (NSA_DH ** -0.5)
        w_g = b_w_in[b][:, qw:].reshape(d, NSA_KV_HEADS, NSA_GROUP, 3).transpose(0, 1, 3, 2).reshape(d, n_gate)
        wt_in = jnp.pad(jnp.concatenate([w_q, w_g], axis=1), ((0, 0), (0, pad_rows))).T.astype(BF16)
        qa, glog = _nsa_inproj(xs, b_norm_w[b], wt_in, slopes)
        oct_, selt, words = _nsa_cmp(qa, kc, vct, cbase)
        words = words[:, :, 0, 0:nwords].reshape(-1)
        o = _nsa_sw(words, qa, glog, oct_, selt, base, ks, vst, kw, vwt)
        xs = _proj_mlp(xs, o, b_w_out[b].astype(BF16), mlp_norm_w[n_a + b],
                       mlp_w_up[n_a + b].astype(BF16), mlp_w_down[n_a + b].astype(BF16),
                       final_norm_w, final=(b == n_b - 1))
    return xs[None]
```

```python
import functools

import jax
import jax.numpy as jnp
from jax import lax
from jax.experimental import pallas as pl
from jax.experimental.pallas import tpu as pltpu

F32 = jnp.float32
BF16 = jnp.bfloat16

NORM_EPS = 1e-6
NEG_INF = -1e30
GATE_FLOOR = 1e-30

HG_HEADS = 8
HG_DK = 128
HG_DV = 128
HG_CHUNK = 64
HG_STEP_CHUNKS = 8

NSA_HEADS = 16
NSA_KV_HEADS = 4
NSA_GROUP = NSA_HEADS // NSA_KV_HEADS
NSA_DH = 64
CMP_BLOCK = 32
CMP_STRIDE = 16
SLC_BLOCK = 64
SLC_TOPK = 16
WINDOW = 512
Q_BLOCK = 128
FORCE_BONUS = 1e4
N_KV_STREAMS = 6

KEY_TILE = 128
SEL_STATIC = 8
SEL_CHUNK = 2
SW_QBLOCKS = 4
CMP_QBLOCKS = 4
QL = NSA_GROUP * Q_BLOCK
AUG = 128
VROWS = NSA_DH + 16

VMEM_LIMIT = 56 * 1024 * 1024


def _params(n_axes, vmem=VMEM_LIMIT):
    return pltpu.CompilerParams(dimension_semantics=("arbitrary",) * n_axes, vmem_limit_bytes=vmem)


def _nt(a, b):
    return lax.dot_general(a, b, (((1,), (1,)), ((), ())), preferred_element_type=F32)


def _tn(a, b):
    return lax.dot_general(a, b, (((0,), (0,)), ((), ())), preferred_element_type=F32)


def _dot(a, b):
    return jnp.dot(a, b, preferred_element_type=F32)


def _rmsnorm(x, w):
    ms = jnp.mean(x * x, axis=-1, keepdims=True)
    return x * lax.rsqrt(ms + NORM_EPS) * w


def _hgrn_kernel(x_ref, nw_ref, w_ref, lbp_ref, gw_ref, seg_ref, o_ref, st_ref, *, layer, n_chunks):
    c = HG_CHUNK
    hk = HG_HEADS * HG_DK

    @pl.when(pl.program_id(0) == 0)
    def _():
        st_ref[...] = jnp.zeros_like(st_ref)

    hn = _rmsnorm(x_ref[...], nw_ref[...]).astype(BF16)
    proj = {grp: _dot(hn, w_ref[:, grp * hk:(grp + 1) * hk]) for grp in (1, 0, 2, 3)}

    a = lbp_ref[...]
    e = jnp.exp(a - jnp.max(a, axis=0, keepdims=True))
    p = e / jnp.sum(e, axis=0, keepdims=True)
    cum = p[0:1]
    for i in range(1, layer + 1):
        cum = cum + p[i:i + 1]
    lb = cum - p[0:1]
    one_m = 1.0 - lb

    sub = lax.broadcasted_iota(jnp.int32, (8, hk), 0)
    ti = lax.broadcasted_iota(jnp.int32, (c, c), 0)
    si = lax.broadcasted_iota(jnp.int32, (c, c), 1)
    masks = [(((ti >> j) & 1) == 1) & (((si >> j) & 1) == 0) & ((ti >> (j + 1)) == (si >> (j + 1)))
             for j in range(6)]
    diag = ti == si
    gw = gw_ref[...]

    for ch in range(n_chunks):
        rows = slice(ch * c, (ch + 1) * c)
        q = proj[0][rows]
        fp = proj[1][rows]
        v = proj[2][rows]
        go = proj[3][rows]

        ea = jnp.exp(-jnp.abs(fp))
        r = 1.0 / (1.0 + ea)
        pos = fp >= 0
        f_gate = lb + one_m * jnp.where(pos, r, ea * r)
        kk = one_m * jnp.where(pos, ea * r, r)
        logf = jnp.log(jnp.maximum(f_gate, GATE_FLOOR))

        hi = logf.astype(BF16)
        rem = logf - hi.astype(F32)
        mid = rem.astype(BF16)
        lo = (rem - mid.astype(F32)).astype(BF16)
        sums = _dot(seg_ref[...], jnp.concatenate([hi, mid, lo], axis=0))
        b = sums[0:c]
        eb = jnp.exp(b[c - 1:c, :])

        xs = []
        for j in range(6):
            ex = jnp.exp(sums[c * (j + 1):c * (j + 2)])
            slabs = []
            for r0 in range(0, c, 8):
                if j < 3:
                    slabs.append(jnp.where(((sub >> j) & 1) == 1, q[r0:r0 + 8], kk[r0:r0 + 8]))
                else:
                    slabs.append(q[r0:r0 + 8] if (r0 >> j) & 1 else kk[r0:r0 + 8])
            xs.append((jnp.concatenate(slabs, axis=0) * ex).astype(BF16))

        vb = v.astype(BF16)
        qs = (q * jnp.exp(b)).astype(BF16)
        khat = (kk * jnp.exp(sums[7 * c:8 * c])).astype(BF16)

        heads = [slice(h * HG_DK, (h + 1) * HG_DK) for h in range(HG_HEADS)]
        qk = q * kk
        attn = [jnp.where(diag, jnp.sum(qk[:, sl], axis=-1, keepdims=True), 0.0) for sl in heads]
        for j in range(6):
            for h, sl in enumerate(heads):
                xj = xs[j][:, sl]
                attn[h] = attn[h] + jnp.where(masks[j], _nt(xj, xj), 0.0)
        outs = []
        for h, sl in enumerate(heads):
            st = st_ref[h]
            outs.append(_dot(attn[h].astype(BF16), vb[:, sl]) + _nt(qs[:, sl], st.astype(BF16)))
            st_ref[h] = st * eb[:, sl] + _tn(vb[:, sl], khat[:, sl])
        for h, sl in enumerate(heads):
            g = go[:, sl]
            o_ref[rows, sl] = (_rmsnorm(outs[h], gw) * (g * (1.0 / (1.0 + jnp.exp(-g))))).astype(o_ref.dtype)


def _segment_matrix():
    c = HG_CHUNK
    t = jnp.arange(c)[:, None]
    i = jnp.arange(c)[None, :]
    parts = [i <= t]
    for j in range(6):
        m = 1 << j
        ref = t - (t % (2 * m)) + m
        parts.append(jnp.where(t >= ref, (i > ref) & (i <= t), (i > t) & (i <= ref)))
    parts.append(i > t)
    return jnp.tile(jnp.concatenate(parts, axis=0), (1, 3)).astype(BF16)


def _hgrn(x, norm_w, w_in, lb_param, gnorm_w, layer):
    t, d = x.shape
    hk = HG_HEADS * HG_DK
    hv = HG_HEADS * HG_DV
    n_layers = lb_param.shape[0]
    seg = _segment_matrix()
    tm = HG_STEP_CHUNKS * HG_CHUNK
    const = lambda i: (0, 0)
    return pl.pallas_call(
        functools.partial(_hgrn_kernel, layer=layer, n_chunks=HG_STEP_CHUNKS),
        grid=(t // tm,),
        in_specs=[pl.BlockSpec((tm, d), lambda i: (i, 0)),
                  pl.BlockSpec((1, d), const),
                  pl.BlockSpec(w_in.shape, const),
                  pl.BlockSpec((n_layers, hk), const),
                  pl.BlockSpec((1, HG_DV), const),
                  pl.BlockSpec(seg.shape, const)],
        out_specs=pl.BlockSpec((tm, hv), lambda i: (i, 0)),
        out_shape=jax.ShapeDtypeStruct((t, hv), BF16),
        scratch_shapes=[pltpu.VMEM((HG_HEADS, HG_DV, HG_DK), F32)],
        compiler_params=_params(1),
        name="hgrn",
    )(x, norm_w.reshape(1, d), w_in, lb_param, gnorm_w.reshape(1, HG_DV), seg)


def _proj_mlp_kernel(x_ref, a_ref, wo_ref, nw_ref, wup_ref, wdn_ref, fnw_ref, o_ref, *, final, ff_tile):
    x1 = x_ref[...] + _dot(a_ref[...], wo_ref[...])
    h = _rmsnorm(x1, nw_ref[...]).astype(BF16)
    acc = x1
    d_ff = wup_ref.shape[1]
    for c0 in range(0, d_ff, ff_tile):
        u = jnp.maximum(_dot(h, wup_ref[:, c0:c0 + ff_tile]), 0.0)
        acc = acc + _dot((u * u).astype(BF16), wdn_ref[c0:c0 + ff_tile, :])
    if final:
        acc = _rmsnorm(acc, fnw_ref[...])
    o_ref[...] = acc


def _proj_mlp(x, a, wo, nw, wup, wdn, fnw, final, tm=1024):
    t, d = x.shape
    d_ff = wup.shape[1]
    const = lambda i: (0, 0)
    once = pl.Buffered(1)
    return pl.pallas_call(
        functools.partial(_proj_mlp_kernel, final=final, ff_tile=min(1024, d_ff)),
        grid=(t // tm,),
        in_specs=[pl.BlockSpec((tm, d), lambda i: (i, 0)),
                  pl.BlockSpec((tm, a.shape[1]), lambda i: (i, 0)),
                  pl.BlockSpec(wo.shape, const, pipeline_mode=once),
                  pl.BlockSpec((1, d), const),
                  pl.BlockSpec(wup.shape, const, pipeline_mode=once),
                  pl.BlockSpec(wdn.shape, const, pipeline_mode=once),
                  pl.BlockSpec((1, d), const)],
        out_specs=pl.BlockSpec((tm, d), lambda i: (i, 0)),
        out_shape=jax.ShapeDtypeStruct((t, d), F32),
        compiler_params=_params(1),
        name="proj_mlp",
    )(x, a, wo, nw.reshape(1, d), wup, wdn, fnw.reshape(1, d))


def _kv_proj_kernel(x_ref, nw_ref, wc_ref, wk_ref, wvt_ref, kc_ref, vc_ref, ks_ref, kw_ref, vst_ref, vwt_ref):
    i = pl.program_id(0)
    tm = x_ref.shape[0]
    gd = NSA_KV_HEADS * NSA_DH
    h = _rmsnorm(x_ref[...], nw_ref[...]).astype(BF16)
    cmp_in = _dot(h, wc_ref[...])
    for g in range(NSA_KV_HEADS):
        kc_ref[g] = cmp_in[:, g * NSA_DH:(g + 1) * NSA_DH]
        vc_ref[g] = cmp_in[:, gd + g * NSA_DH:gd + (g + 1) * NSA_DH]
    kk = _dot(h, wk_ref[...])
    row = lax.broadcasted_iota(jnp.int32, (tm, AUG), 0)
    lane = lax.broadcasted_iota(jnp.int32, (tm, AUG), 1)
    tok = i * tm + row
    tile = (tok // KEY_TILE).astype(F32)
    offs = (tok % KEY_TILE).astype(F32)
    consts = jnp.where((lane >= NSA_DH) & (lane < NSA_DH + 3), tile,
                       jnp.where((lane >= NSA_DH + 3) & (lane < NSA_DH + 6), 1.0,
                                 jnp.where((lane >= NSA_DH + 6) & (lane < NSA_DH + 9), offs, 0.0)))
    for g in range(NSA_KV_HEADS):
        ks_ref[g] = (kk[:, g * AUG:(g + 1) * AUG] + consts).astype(BF16)
        kw_ref[g] = (kk[:, (NSA_KV_HEADS + g) * AUG:(NSA_KV_HEADS + g + 1) * AUG] + consts).astype(BF16)
    vt = _nt(wvt_ref[...], h)
    r16 = lax.broadcasted_iota(jnp.int32, (VROWS - NSA_DH, tm), 0)
    ones_blk = jnp.where(r16 == 0, 1.0, 0.0).astype(BF16)
    for g in range(NSA_KV_HEADS):
        vst_ref[g, 0:NSA_DH, :] = vt[g * NSA_DH:(g + 1) * NSA_DH, :].astype(BF16)
        vst_ref[g, NSA_DH:VROWS, :] = ones_blk
        vwt_ref[g, 0:NSA_DH, :] = vt[gd + g * NSA_DH:gd + (g + 1) * NSA_DH, :].astype(BF16)
        vwt_ref[g, NSA_DH:VROWS, :] = ones_blk


def _kv_proj(x, nw, wc, wk, wvt, tm=256):
    t, d = x.shape
    g = NSA_KV_HEADS
    const = lambda i: (0, 0)
    rows = lambda i: (0, i, 0)
    cols = lambda i: (0, 0, i)
    return pl.pallas_call(
        _kv_proj_kernel,
        grid=(t // tm,),
        in_specs=[pl.BlockSpec((tm, d), lambda i: (i, 0)),
                  pl.BlockSpec((1, d), const),
                  pl.BlockSpec(wc.shape, const),
                  pl.BlockSpec(wk.shape, const),
                  pl.BlockSpec(wvt.shape, const)],
        out_specs=[pl.BlockSpec((g, tm, NSA_DH), rows), pl.BlockSpec((g, tm, NSA_DH), rows),
                   pl.BlockSpec((g, tm, AUG), rows), pl.BlockSpec((g, tm, AUG), rows),
                   pl.BlockSpec((g, VROWS, tm), cols), pl.BlockSpec((g, VROWS, tm), cols)],
        out_shape=[jax.ShapeDtypeStruct((g, t, NSA_DH), F32), jax.ShapeDtypeStruct((g, t, NSA_DH), F32),
                   jax.ShapeDtypeStruct((g, t, AUG), BF16), jax.ShapeDtypeStruct((g, t, AUG), BF16),
                   jax.ShapeDtypeStruct((g, VROWS, t), BF16), jax.ShapeDtypeStruct((g, VROWS, t), BF16)],
        compiler_params=_params(1),
        name="kv_proj",
    )(x, nw.reshape(1, d), wc, wk, wvt)


def _nsa_inproj_kernel(x_ref, nw_ref, wt_ref, sl_ref, q_ref, gl_ref, *, nqb):
    i = pl.program_id(0)
    qw = NSA_HEADS * NSA_DH
    h = _rmsnorm(x_ref[...], nw_ref[...]).astype(BF16)
    pt = _nt(wt_ref[...], h)
    r16 = lax.broadcasted_iota(jnp.int32, (16, QL), 0)
    ql = (lax.broadcasted_iota(jnp.int32, (1, QL), 1) & (Q_BLOCK - 1)).astype(F32)

    def split3(v):
        a = v.astype(BF16).astype(F32)
        b = (v - a).astype(BF16).astype(F32)
        return a, b, v - a - b

    for g in range(NSA_KV_HEADS):
        slope = sl_ref[g]
        s1, s2, s3 = split3(slope * float(KEY_TILE))
        r1, r2, r3 = split3(slope)
        for qb in range(nqb):
            lanes = slice(qb * Q_BLOCK, (qb + 1) * Q_BLOCK)
            for j in range(NSA_GROUP):
                r0 = (g * NSA_GROUP + j) * NSA_DH
                q_ref[g, qb, 0:NSA_DH, j * Q_BLOCK:(j + 1) * Q_BLOCK] = pt[r0:r0 + NSA_DH, lanes].astype(BF16)
            tq = ((i * nqb + qb) * Q_BLOCK).astype(F32) + ql
            o1, o2, o3 = split3(-(slope * tq))
            aug = jnp.zeros((16, QL), F32)
            for k, term in enumerate((s1, s2, s3, o1, o2, o3, r1, r2, r3)):
                aug = jnp.where(r16 == k, term, aug)
            q_ref[g, qb, NSA_DH:NSA_DH + 16, :] = aug.astype(BF16)
            q_ref[g, qb, NSA_DH + 16:AUG, :] = jnp.zeros((AUG - NSA_DH - 16, QL), BF16)
            for c in range(3):
                for j in range(NSA_GROUP):
                    r0 = qw + (g * 3 + c) * NSA_GROUP + j
                    gl_ref[g, qb, c:c + 1, j * Q_BLOCK:(j + 1) * Q_BLOCK] = pt[r0:r0 + 1, lanes]


def _nsa_inproj(x, nw, wt, slopes, nqb=2):
    t, d = x.shape
    nq = t // Q_BLOCK
    g = NSA_KV_HEADS
    tm = nqb * Q_BLOCK
    return pl.pallas_call(
        functools.partial(_nsa_inproj_kernel, nqb=nqb),
        grid=(t // tm,),
        in_specs=[pl.BlockSpec((tm, d), lambda i: (i, 0)),
                  pl.BlockSpec((1, d), lambda i: (0, 0)),
                  pl.BlockSpec(wt.shape, lambda i: (0, 0)),
                  pl.BlockSpec(slopes.shape, lambda i: (0, 0, 0))],
        out_specs=[pl.BlockSpec((g, nqb, AUG, QL), lambda i: (0, i, 0, 0)),
                   pl.BlockSpec((g, nqb, 3, QL), lambda i: (0, i, 0, 0))],
        out_shape=[jax.ShapeDtypeStruct((g, nq, AUG, QL), BF16),
                   jax.ShapeDtypeStruct((g, nq, 3, QL), F32)],
        compiler_params=_params(1),
        name="nsa_inproj",
    )(x, nw.reshape(1, d), wt, slopes)


def _compress_kernel(hf_ref, pe_ref, w1_ref, w2_ref, o_ref):
    x = hf_ref[0, 0]
    nc, half = x.shape
    pe_lo = pe_ref[0, 0:1, :]
    pe_hi = pe_ref[0, 1:2, :]
    w1_lo = w1_ref[0, 0:half, :]
    w1_hi = w1_ref[0, half:2 * half, :]
    u = _dot((x + pe_lo).astype(BF16), w1_lo)
    vv = _dot((x + pe_hi).astype(BF16), w1_hi)
    nxt = pltpu.roll(vv, nc - 1, 0)
    pad = _dot(jnp.broadcast_to(pe_hi, (8, half)).astype(BF16), w1_hi)[0:1]
    row = lax.broadcasted_iota(jnp.int32, vv.shape, 0)
    pre = u + jnp.where(row == nc - 1, pad, nxt)
    hid = 0.5 * pre * (1.0 + jnp.tanh(0.7978845608028654 * (pre + 0.044715 * (pre * pre * pre))))
    o_ref[0, 0] = _dot(hid.astype(BF16), w2_ref[0])


def _compress(hf, pe, w1, w2):
    s, g, nc, half = hf.shape
    return pl.pallas_call(
        _compress_kernel,
        grid=(s, g),
        in_specs=[pl.BlockSpec((1, 1, nc, half), lambda i, j: (i, j, 0, 0)),
                  pl.BlockSpec((1, 2, half), lambda i, j: (i, 0, 0)),
                  pl.BlockSpec((1,) + w1.shape[1:], lambda i, j: (i, 0, 0)),
                  pl.BlockSpec((1,) + w2.shape[1:], lambda i, j: (i, 0, 0))],
        out_specs=pl.BlockSpec((1, 1, nc, w2.shape[2]), lambda i, j: (i, j, 0, 0)),
        out_shape=jax.ShapeDtypeStruct((s, g, nc, w2.shape[2]), F32),
        compiler_params=_params(2),
        name="compress",
    )(hf, pe, w1, w2)


def _nsa_cmp_kernel(q_ref, kc_ref, vct_ref, cb_ref, oc_ref, sel_ref, l_ref, p_ref, *, ns, topk, nqb):
    ratio = SLC_BLOCK // CMP_STRIDE
    qis = [pl.program_id(1) * nqb + qb for qb in range(nqb)]

    def attend(nv):
        cb = cb_ref[0:nv, :]
        n_i = lax.broadcasted_iota(jnp.int32, (nv, Q_BLOCK), 0)
        ql = lax.broadcasted_iota(jnp.int32, (nv, Q_BLOCK), 1)
        n_f = n_i.astype(F32)
        mxs = []
        for qb in range(nqb):
            qa = q_ref[0, qb]
            t0f = (qis[qb] * Q_BLOCK).astype(F32)
            mx = jnp.full((1, QL), NEG_INF, F32)
            for u in range(ratio):
                s = _dot(kc_ref[0, u * ns:u * ns + nv, :], qa)
                s = jnp.where(cb >= float(CMP_STRIDE * u + CMP_BLOCK - 1) - t0f, s, NEG_INF)
                p_ref[qb, u * ns:u * ns + nv, :] = s
                mx = jnp.maximum(mx, jnp.max(s, axis=0, keepdims=True))
            mxs.append(mx)
        cands, forceds = [], []
        for qb in range(nqb):
            mx = mxs[qb]
            valid = mx > 0.5 * NEG_INF
            acc = jnp.zeros((VROWS, QL), F32)
            head_pool = [None] * NSA_GROUP
            for u in range(ratio):
                p = jnp.exp(p_ref[qb, u * ns:u * ns + nv, :] - mx)
                acc = acc + _dot(vct_ref[0, :, u * ns:u * ns + nv], p.astype(BF16))
                for j in range(NSA_GROUP):
                    pj = p[:, j * Q_BLOCK:(j + 1) * Q_BLOCK]
                    head_pool[j] = pj if head_pool[j] is None else head_pool[j] + pj
                    if u == ratio - 1:
                        head_pool[j] = head_pool[j] + jnp.where(n_i >= 1, pltpu.roll(pj, 1, 0), 0.0)
            inv = jnp.where(valid, 1.0 / acc[NSA_DH:NSA_DH + 1], 0.0)
            oc_ref[0, qb] = acc[0:NSA_DH] * inv
            pooled = head_pool[0] * inv[:, 0:Q_BLOCK]
            for j in range(1, NSA_GROUP):
                pooled = pooled + head_pool[j] * inv[:, j * Q_BLOCK:(j + 1) * Q_BLOCK]
            cur = 2 * qis[qb] + (ql >= SLC_BLOCK).astype(jnp.int32)
            ok = n_i <= cur
            forced = ok & ((n_i == 0) | (n_i == cur) | (n_i == cur - 1))
            forceds.append(forced)
            cands.append(jnp.where(ok & jnp.logical_not(forced), pooled, NEG_INF))
        scores = list(cands)
        for _ in range(topk - 3):
            for qb in range(nqb):
                best = jnp.max(scores[qb], axis=0, keepdims=True)
                idx = jnp.min(jnp.where(scores[qb] == best, n_f, float(ns)), axis=0, keepdims=True)
                scores[qb] = jnp.where(n_f == idx, -jnp.inf, scores[qb])
        nt = ns // 2
        pair = jnp.where((lax.broadcasted_iota(jnp.int32, (nv, 128), 0) >> 1)
                         == lax.broadcasted_iota(jnp.int32, (nv, 128), 1), 1.0, 0.0)
        lane = lax.broadcasted_iota(jnp.int32, (1, 128), 1)
        before = (lax.broadcasted_iota(jnp.int32, (128, 128), 0)
                  < lax.broadcasted_iota(jnp.int32, (128, 128), 1)).astype(BF16)
        k_f = lax.broadcasted_iota(jnp.int32, (nt, 128), 0).astype(F32)
        for qb in range(nqb):
            picked = (scores[qb] == -jnp.inf) & (cands[qb] > 0.5 * NEG_INF)
            sel = jnp.where(forceds[qb] | picked, 1.0, 0.0)
            sel_ref[0, qb, 0:nv, :] = sel
            if nv < ns:
                sel_ref[0, qb, nv:ns, :] = jnp.zeros((ns - nv, Q_BLOCK), F32)
            any_q = jnp.max(sel, axis=1, keepdims=True)
            active = (jnp.sum(any_q * pair, axis=0, keepdims=True) > 0.5) & (lane < qis[qb])
            act = jnp.where(active, 1.0, 0.0)
            slot = _dot(jnp.broadcast_to(act, (8, 128)).astype(BF16), before)[0:1]
            hit = jnp.where((k_f == slot) & active, lane.astype(F32), 0.0)
            lst = jnp.sum(hit, axis=1, keepdims=True)
            cnt = jnp.sum(act, axis=1, keepdims=True)
            lst = jnp.where(lax.broadcasted_iota(jnp.int32, (nt, 1), 0) == nt - 1, cnt, lst)
            l_ref[0, qb] = lst.astype(jnp.int32)

    n_cls = 8
    step = ns // n_cls
    need = 2 * qis[-1] + 2
    for k in range(n_cls):
        lo, hi = k * step, (k + 1) * step

        @pl.when((need > lo) & (need <= hi))
        def _():
            attend(hi)


def _nsa_cmp(qa, kc, vct, cbase):
    g, nq, _, _ = qa.shape
    dh = NSA_DH
    ncmp = kc.shape[1]
    ns = ncmp // (SLC_BLOCK // CMP_STRIDE)
    topk = min(SLC_TOPK, ns)
    nqb = CMP_QBLOCKS
    return pl.pallas_call(
        functools.partial(_nsa_cmp_kernel, ns=ns, topk=topk, nqb=nqb),
        grid=(g, nq // nqb),
        in_specs=[pl.BlockSpec((1, nqb, AUG, QL), lambda i, j: (i, j, 0, 0)),
                  pl.BlockSpec((1, ncmp, AUG), lambda i, j: (i, 0, 0)),
                  pl.BlockSpec((1, VROWS, ncmp), lambda i, j: (i, 0, 0)),
                  pl.BlockSpec((ns, QL), lambda i, j: (0, 0))],
        out_specs=[pl.BlockSpec((1, nqb, dh, QL), lambda i, j: (i, j, 0, 0)),
                   pl.BlockSpec((1, nqb, ns, Q_BLOCK), lambda i, j: (i, j, 0, 0)),
                   pl.BlockSpec((1, nqb, ns // 2, 1), lambda i, j: (i, j, 0, 0))],
        out_shape=[jax.ShapeDtypeStruct((g, nq, dh, QL), F32),
                   jax.ShapeDtypeStruct((g, nq, ns, Q_BLOCK), F32),
                   jax.ShapeDtypeStruct((g, nq, ns // 2, 1), jnp.int32)],
        scratch_shapes=[pltpu.VMEM((nqb, ncmp, QL), F32)],
        compiler_params=_params(2),
        name="nsa_cmp",
    )(qa, kc, vct, cbase)


def _nsa_sw_kernel(lists_ref, q_ref, gl_ref, oc_ref, sel_ref, base_ref, ks_ref, vst_ref,
                   kw_ref, vwt_ref, o_ref, m_ref, acc_ref, *, nq, nqb):
    g = pl.program_id(0)
    step = pl.program_id(1)
    half = SLC_BLOCK
    tile_base = base_ref[...]

    def probs(s, mx):
        return jnp.exp((s - mx).astype(BF16))

    def sel_mask(ra, rb):
        mk = jnp.concatenate([jnp.broadcast_to(ra, (half, Q_BLOCK)), jnp.broadcast_to(rb, (half, Q_BLOCK))], axis=0)
        return jnp.concatenate([mk] * NSA_GROUP, axis=1) > 0.5

    def listed_tile(qb, cnt, slot):
        live = slot < cnt
        m = jnp.where(live, lists_ref[(g * nq + qis[qb]) * nq + jnp.minimum(slot, nq - 2)], 0)
        keep = jnp.where(live, 1.0, 0.0)
        pos = pl.multiple_of(m * KEY_TILE, KEY_TILE)
        mask = sel_mask(sel_ref[0, qb, pl.ds(2 * m, 1), :] * keep, sel_ref[0, qb, pl.ds(2 * m + 1, 1), :] * keep)
        return ks_ref[0, pl.ds(pos, KEY_TILE), :], vst_ref[0, :, pl.ds(pos, KEY_TILE)], mask

    def masked_scores(qa, k_tiles, masks):
        sc = _dot(jnp.concatenate(k_tiles, axis=0), qa)
        tiles, mx = [], None
        for u, mask in enumerate(masks):
            s = jnp.where(mask, sc[u * KEY_TILE:(u + 1) * KEY_TILE, :], NEG_INF)
            mu = jnp.max(s, axis=0, keepdims=True)
            mx = mu if mx is None else jnp.maximum(mx, mu)
            tiles.append(s)
        return tiles, mx

    def pv(s_tiles, v_tiles, mx):
        return _dot(jnp.concatenate(v_tiles, axis=1), jnp.concatenate([probs(s, mx) for s in s_tiles], axis=0))

    qis = [step * nqb + qb for qb in range(nqb)]
    cnts = [lists_ref[(g * nq + qi) * nq + nq - 1] for qi in qis]

    n_wt = (WINDOW + Q_BLOCK) // KEY_TILE
    staged = []
    for qb in range(nqb):
        qi, cnt = qis[qb], cnts[qb]
        t0 = qi * Q_BLOCK
        qa = q_ref[0, qb]
        wk, wv, wmask = [], [], []
        for u in range(n_wt):
            pos_raw = t0 - WINDOW + u * KEY_TILE
            pos = pl.multiple_of(jnp.maximum(pos_raw, 0), KEY_TILE)
            wk.append(kw_ref[0, pl.ds(pos, KEY_TILE), :])
            wv.append(vwt_ref[0, :, pl.ds(pos, KEY_TILE)])
            live = pos_raw >= 0
            if u == 0:
                wmask.append(jnp.logical_and(tile_base < 0, live))
            elif u == n_wt - 1:
                wmask.append(tile_base >= 0)
            else:
                wmask.append(live)
        posd = pl.multiple_of(t0, KEY_TILE)
        diag_mask = sel_mask(sel_ref[0, qb, pl.ds(2 * qi, 1), :],
                             sel_ref[0, qb, pl.ds(2 * qi + 1, 1), :]) & (tile_base >= 0)
        listed = [listed_tile(qb, cnt, jnp.int32(u)) for u in range(SEL_STATIC)]
        stiles, m0 = masked_scores(qa, [ks_ref[0, pl.ds(posd, KEY_TILE), :]] + [t[0] for t in listed],
                                   [diag_mask] + [t[2] for t in listed])
        wtiles, mw = masked_scores(qa, wk, wmask)
        staged.append((stiles, m0, [vst_ref[0, :, pl.ds(posd, KEY_TILE)]] + [t[1] for t in listed], wtiles, mw, wv))

    o_wins = []
    for qb in range(nqb):
        stiles, m0, svals, wtiles, mw, wv = staged[qb]
        m_ref[qb] = m0
        acc_ref[qb] = pv(stiles, svals, m0)
        accw = pv(wtiles, wv, mw)
        o_wins.append(accw[0:NSA_DH] * (1.0 / accw[NSA_DH:NSA_DH + 1]))

    for qb in range(nqb):
        cnt = cnts[qb]
        qa = q_ref[0, qb]

        def chunk_body(ci, carry, qb=qb, cnt=cnt, qa=qa):
            listed = [listed_tile(qb, cnt, SEL_STATIC + ci * SEL_CHUNK + u) for u in range(SEL_CHUNK)]
            tiles, cmax = masked_scores(qa, [t[0] for t in listed], [t[2] for t in listed])
            m_old = m_ref[qb]
            m_new = jnp.maximum(m_old, cmax)
            acc_ref[qb] = jnp.exp(m_old - m_new) * acc_ref[qb] + pv(tiles, [t[1] for t in listed], m_new)
            m_ref[qb] = m_new
            return carry

        lax.fori_loop(0, (jnp.maximum(cnt - SEL_STATIC, 0) + SEL_CHUNK - 1) // SEL_CHUNK, chunk_body, 0)
        o_sel = acc_ref[qb, 0:NSA_DH, :] * (1.0 / acc_ref[qb, NSA_DH:NSA_DH + 1, :])

        gl = gl_ref[0, qb]
        gate = 1.0 / (1.0 + jnp.exp(-gl))
        out_t = gate[0:1] * oc_ref[0, qb] + gate[1:2] * o_sel + gate[2:3] * o_wins[qb]
        halves = []
        for j in range(0, NSA_GROUP, 2):
            pair = jnp.concatenate([out_t[:, j * Q_BLOCK:(j + 1) * Q_BLOCK],
                                    out_t[:, (j + 1) * Q_BLOCK:(j + 2) * Q_BLOCK]], axis=0)
            halves.append(pair.T)
        o_ref[qb * Q_BLOCK:(qb + 1) * Q_BLOCK, :] = jnp.concatenate(halves, axis=1).astype(o_ref.dtype)


def _nsa_sw(lists, qa, glog, oct_, selt, base, ks, vst, kw, vwt):
    g, nq, _, _ = qa.shape
    dh = NSA_DH
    t = ks.shape[1]
    ns = selt.shape[2]
    assert lists.shape[0] == g * nq * nq
    nqb = SW_QBLOCKS
    grid_spec = pltpu.PrefetchScalarGridSpec(
        num_scalar_prefetch=1,
        grid=(g, nq // nqb),
        in_specs=[pl.BlockSpec((1, nqb, AUG, QL), lambda i, j, w: (i, j, 0, 0)),
                  pl.BlockSpec((1, nqb, 3, QL), lambda i, j, w: (i, j, 0, 0)),
                  pl.BlockSpec((1, nqb, dh, QL), lambda i, j, w: (i, j, 0, 0)),
                  pl.BlockSpec((1, nqb, ns, Q_BLOCK), lambda i, j, w: (i, j, 0, 0)),
                  pl.BlockSpec((KEY_TILE, QL), lambda i, j, w: (0, 0)),
                  pl.BlockSpec((1, t, AUG), lambda i, j, w: (i, 0, 0)),
                  pl.BlockSpec((1, VROWS, t), lambda i, j, w: (i, 0, 0)),
                  pl.BlockSpec((1, t, AUG), lambda i, j, w: (i, 0, 0)),
                  pl.BlockSpec((1, VROWS, t), lambda i, j, w: (i, 0, 0))],
        out_specs=pl.BlockSpec((nqb * Q_BLOCK, NSA_GROUP * dh), lambda i, j, w: (j, i)),
        scratch_shapes=[pltpu.VMEM((nqb, 1, QL), F32), pltpu.VMEM((nqb, VROWS, QL), F32)],
    )
    return pl.pallas_call(
        functools.partial(_nsa_sw_kernel, nq=nq, nqb=nqb),
        grid_spec=grid_spec,
        out_shape=jax.ShapeDtypeStruct((t, g * NSA_GROUP * dh), BF16),
        compiler_params=_params(2),
        name="nsa_sw",
    )(lists, qa, glog, oct_, selt, base, ks, vst, kw, vwt)


def kernel(x, a_norm_w, a_w_in, a_gnorm_w, a_w_out, a_lower_bounds, kv_norm_w, kv_w, cmp_pe_k, cmp_w1_k,
           cmp_w2_k, cmp_pe_v, cmp_w1_v, cmp_w2_v, b_norm_w, b_w_in, b_w_out, mlp_norm_w, mlp_w_up,
           mlp_w_down, final_norm_w):
    bsz, t, d = x.shape
    assert bsz == 1 and t % Q_BLOCK == 0 and t >= WINDOW + Q_BLOCK
    n_a = a_w_in.shape[0]
    n_b = b_w_in.shape[0]
    nq = t // Q_BLOCK
    xs = x[0]

    for layer in range(n_a):
        o = _hgrn(xs, a_norm_w[layer], a_w_in[layer].astype(BF16), a_lower_bounds, a_gnorm_w[layer], layer)
        xs = _proj_mlp(xs, o, a_w_out[layer].astype(BF16), mlp_norm_w[layer],
                       mlp_w_up[layer].astype(BF16), mlp_w_down[layer].astype(BF16),
                       final_norm_w, final=False)

    gd = NSA_KV_HEADS * NSA_DH
    kvw = kv_w.reshape(d, N_KV_STREAMS, NSA_KV_HEADS, NSA_DH)
    wc = kvw[:, 0:2].reshape(d, 2 * gd).astype(BF16)
    wk = jnp.pad(jnp.stack([kvw[:, 2], kvw[:, 4]], axis=1), ((0, 0), (0, 0), (0, 0), (0, AUG - NSA_DH)))
    wk = wk.reshape(d, 2 * NSA_KV_HEADS * AUG).astype(BF16)
    wvt = jnp.stack([kvw[:, 3], kvw[:, 5]], axis=1).reshape(d, 2 * gd).T.astype(BF16)
    kc_in, vc_in, ks, kw, vst, vwt = _kv_proj(xs, kv_norm_w, wc, wk, wvt)

    ncmp = t // CMP_STRIDE
    half = CMP_STRIDE * NSA_DH
    hf = jnp.stack([kc_in.reshape(NSA_KV_HEADS, ncmp, half), vc_in.reshape(NSA_KV_HEADS, ncmp, half)])
    pe = jnp.stack([cmp_pe_k.reshape(2, half), cmp_pe_v.reshape(2, half)])
    w1 = jnp.stack([cmp_w1_k, cmp_w1_v]).astype(BF16)
    w2 = jnp.stack([cmp_w2_k, cmp_w2_v]).astype(BF16)
    cmp_out = _compress(hf, pe, w1, w2)
    ratio = SLC_BLOCK // CMP_STRIDE
    ns = ncmp // ratio
    perm = cmp_out.reshape(2, NSA_KV_HEADS, ns, ratio, NSA_DH).transpose(0, 1, 3, 2, 4).reshape(
        2, NSA_KV_HEADS, ncmp, NSA_DH)
    crow = jnp.arange(ncmp)
    cmp_end = SLC_BLOCK * (crow % ns) + CMP_STRIDE * (crow // ns) + (CMP_BLOCK - 1)
    aug_cols = jnp.stack([cmp_end // KEY_TILE] * 3 + [jnp.ones_like(cmp_end)] * 3 + [cmp_end % KEY_TILE] * 3,
                         axis=1).astype(F32)
    aug_cols = jnp.pad(aug_cols, ((0, 0), (0, AUG - NSA_DH - aug_cols.shape[1])))
    kc = jnp.concatenate([perm[0], jnp.broadcast_to(aug_cols, (NSA_KV_HEADS, ncmp, AUG - NSA_DH))],
                         axis=2).astype(BF16)
    ones_rows = jnp.zeros((NSA_KV_HEADS, VROWS - NSA_DH, ncmp), F32).at[:, 0, :].set(1.0)
    vct = jnp.concatenate([perm[1].transpose(0, 2, 1), ones_rows], axis=1).astype(BF16)

    slopes = jnp.exp2(-8.0 * jnp.arange(1, NSA_HEADS + 1, dtype=F32) / NSA_HEADS)
    slopes = jnp.repeat(slopes.reshape(NSA_KV_HEADS, 1, NSA_GROUP), Q_BLOCK, axis=2)
    q_off = jnp.tile(jnp.arange(Q_BLOCK, dtype=F32), NSA_GROUP)[None, :]
    base = q_off - jnp.arange(KEY_TILE, dtype=F32)[:, None]
    cbase = q_off - float(SLC_BLOCK) * jnp.arange(ns, dtype=F32)[:, None]

    qw = NSA_HEADS * NSA_DH
    n_gate = b_w_in.shape[2] - qw
    pad_rows = (-(qw + n_gate)) % 128
    for b in range(n_b):
        w_q = b_w_in[b][:, 0:qw] * (NSA_DH ** -0.5)
        w_g = b_w_in[b][:, qw:].reshape(d, NSA_KV_HEADS, NSA_GROUP, 3).transpose(0, 1, 3, 2).reshape(d, n_gate)
        wt_in = jnp.pad(jnp.concatenate([w_q, w_g], axis=1), ((0, 0), (0, pad_rows))).T.astype(BF16)
        qa, glog = _nsa_inproj(xs, b_norm_w[b], wt_in, slopes)
        oct_, selt, lists = _nsa_cmp(qa, kc, vct, cbase)
        o = _nsa_sw(lists.reshape(-1), qa, glog, oct_, selt, base, ks, vst, kw, vwt)
        xs = _proj_mlp(xs, o, b_w_out[b].astype(BF16), mlp_norm_w[n_a + b],
                       mlp_w_up[n_a + b].astype(BF16), mlp_w_down[n_a + b].astype(BF16),
                       final_norm_w, final=(b == n_b - 1))
    return xs[None]
```

```python
import functools

import jax
import jax.numpy as jnp
from jax import lax
from jax.experimental import pallas as pl
from jax.experimental.pallas import tpu as pltpu

F32 = jnp.float32
BF16 = jnp.bfloat16

NORM_EPS = 1e-6
NEG_INF = -1e30
GATE_FLOOR = 1e-30

HG_HEADS = 8
HG_DK = 128
HG_DV = 128
HG_CHUNK = 64
HG_STEP_CHUNKS = 8

NSA_HEADS = 16
NSA_KV_HEADS = 4
NSA_GROUP = NSA_HEADS // NSA_KV_HEADS
NSA_DH = 64
CMP_BLOCK = 32
CMP_STRIDE = 16
SLC_BLOCK = 64
SLC_TOPK = 16
WINDOW = 512
Q_BLOCK = 128
FORCE_BONUS = 1e4
N_KV_STREAMS = 6

KEY_TILE = 128
SEL_STATIC = 8
SEL_CHUNK = 2
SW_QBLOCKS = 4
CMP_QBLOCKS = 4
QL = NSA_GROUP * Q_BLOCK
AUG = 128
VROWS = NSA_DH + 16

VMEM_LIMIT = 56 * 1024 * 1024


def _params(n_axes, vmem=VMEM_LIMIT):
    return pltpu.CompilerParams(dimension_semantics=("arbitrary",) * n_axes, vmem_limit_bytes=vmem)


def _nt(a, b):
    return lax.dot_general(a, b, (((1,), (1,)), ((), ())), preferred_element_type=F32)


def _tn(a, b):
    return lax.dot_general(a, b, (((0,), (0,)), ((), ())), preferred_element_type=F32)


def _dot(a, b):
    return jnp.dot(a, b, preferred_element_type=F32)


def _rmsnorm(x, w):
    ms = jnp.mean(x * x, axis=-1, keepdims=True)
    return x * lax.rsqrt(ms + NORM_EPS) * w


def _hgrn_kernel(x_ref, nw_ref, w_ref, lbp_ref, gw_ref, seg_ref, o_ref, st_ref, *, layer, n_chunks):
    c = HG_CHUNK
    hk = HG_HEADS * HG_DK

    @pl.when(pl.program_id(0) == 0)
    def _():
        st_ref[...] = jnp.zeros_like(st_ref)

    hn = _rmsnorm(x_ref[...], nw_ref[...]).astype(BF16)
    proj = {grp: _dot(hn, w_ref[:, grp * hk:(grp + 1) * hk]) for grp in (1, 0, 2, 3)}

    a = lbp_ref[...]
    e = jnp.exp(a - jnp.max(a, axis=0, keepdims=True))
    p = e / jnp.sum(e, axis=0, keepdims=True)
    cum = p[0:1]
    for i in range(1, layer + 1):
        cum = cum + p[i:i + 1]
    lb = cum - p[0:1]
    one_m = 1.0 - lb

    sub = lax.broadcasted_iota(jnp.int32, (8, hk), 0)
    ti = lax.broadcasted_iota(jnp.int32, (c, c), 0)
    si = lax.broadcasted_iota(jnp.int32, (c, c), 1)
    masks = [(((ti >> j) & 1) == 1) & (((si >> j) & 1) == 0) & ((ti >> (j + 1)) == (si >> (j + 1)))
             for j in range(6)]
    diag = ti == si
    gw = gw_ref[...]

    for ch in range(n_chunks):
        rows = slice(ch * c, (ch + 1) * c)
        q = proj[0][rows]
        fp = proj[1][rows]
        v = proj[2][rows]
        go = proj[3][rows]

        ea = jnp.exp(-jnp.abs(fp))
        r = 1.0 / (1.0 + ea)
        pos = fp >= 0
        f_gate = lb + one_m * jnp.where(pos, r, ea * r)
        kk = one_m * jnp.where(pos, ea * r, r)
        logf = jnp.log(jnp.maximum(f_gate, GATE_FLOOR))

        hi = logf.astype(BF16)
        rem = logf - hi.astype(F32)
        mid = rem.astype(BF16)
        lo = (rem - mid.astype(F32)).astype(BF16)
        sums = _dot(seg_ref[...], jnp.concatenate([hi, mid, lo], axis=0))
        b = sums[0:c]
        eb = jnp.exp(b[c - 1:c, :])

        xs = []
        for j in range(6):
            ex = jnp.exp(sums[c * (j + 1):c * (j + 2)])
            slabs = []
            for r0 in range(0, c, 8):
                if j < 3:
                    slabs.append(jnp.where(((sub >> j) & 1) == 1, q[r0:r0 + 8], kk[r0:r0 + 8]))
                else:
                    slabs.append(q[r0:r0 + 8] if (r0 >> j) & 1 else kk[r0:r0 + 8])
            xs.append((jnp.concatenate(slabs, axis=0) * ex).astype(BF16))

        vb = v.astype(BF16)
        qs = (q * jnp.exp(b)).astype(BF16)
        khat = (kk * jnp.exp(sums[7 * c:8 * c])).astype(BF16)

        heads = [slice(h * HG_DK, (h + 1) * HG_DK) for h in range(HG_HEADS)]
        qk = q * kk
        attn = [jnp.where(diag, jnp.sum(qk[:, sl], axis=-1, keepdims=True), 0.0) for sl in heads]
        for j in range(6):
            for h, sl in enumerate(heads):
                xj = xs[j][:, sl]
                attn[h] = attn[h] + jnp.where(masks[j], _nt(xj, xj), 0.0)
        outs = []
        for h, sl in enumerate(heads):
            st = st_ref[h]
            outs.append(_dot(attn[h].astype(BF16), vb[:, sl]) + _nt(qs[:, sl], st.astype(BF16)))
            st_ref[h] = st * eb[:, sl] + _tn(vb[:, sl], khat[:, sl])
        for h, sl in enumerate(heads):
            g = go[:, sl]
            o_ref[rows, sl] = (_rmsnorm(outs[h], gw) * (g * (1.0 / (1.0 + jnp.exp(-g))))).astype(o_ref.dtype)


def _segment_matrix():
    c = HG_CHUNK
    t = jnp.arange(c)[:, None]
    i = jnp.arange(c)[None, :]
    parts = [i <= t]
    for j in range(6):
        m = 1 << j
        ref = t - (t % (2 * m)) + m
        parts.append(jnp.where(t >= ref, (i > ref) & (i <= t), (i > t) & (i <= ref)))
    parts.append(i > t)
    return jnp.tile(jnp.concatenate(parts, axis=0), (1, 3)).astype(BF16)


def _hgrn(x, norm_w, w_in, lb_param, gnorm_w, layer):
    t, d = x.shape
    hk = HG_HEADS * HG_DK
    hv = HG_HEADS * HG_DV
    n_layers = lb_param.shape[0]
    seg = _segment_matrix()
    tm = HG_STEP_CHUNKS * HG_CHUNK
    const = lambda i: (0, 0)
    return pl.pallas_call(
        functools.partial(_hgrn_kernel, layer=layer, n_chunks=HG_STEP_CHUNKS),
        grid=(t // tm,),
        in_specs=[pl.BlockSpec((tm, d), lambda i: (i, 0)),
                  pl.BlockSpec((1, d), const),
                  pl.BlockSpec(w_in.shape, const),
                  pl.BlockSpec((n_layers, hk), const),
                  pl.BlockSpec((1, HG_DV), const),
                  pl.BlockSpec(seg.shape, const)],
        out_specs=pl.BlockSpec((tm, hv), lambda i: (i, 0)),
        out_shape=jax.ShapeDtypeStruct((t, hv), BF16),
        scratch_shapes=[pltpu.VMEM((HG_HEADS, HG_DV, HG_DK), F32)],
        compiler_params=_params(1),
        name="hgrn",
    )(x, norm_w.reshape(1, d), w_in, lb_param, gnorm_w.reshape(1, HG_DV), seg)


def _proj_mlp_kernel(x_ref, a_ref, wo_ref, nw_ref, wup_ref, wdn_ref, fnw_ref, o_ref, *, final, ff_tile):
    x1 = x_ref[...] + _dot(a_ref[...], wo_ref[...])
    h = _rmsnorm(x1, nw_ref[...]).astype(BF16)
    acc = x1
    d_ff = wup_ref.shape[1]
    for c0 in range(0, d_ff, ff_tile):
        u = jnp.maximum(_dot(h, wup_ref[:, c0:c0 + ff_tile]), 0.0)
        acc = acc + _dot((u * u).astype(BF16), wdn_ref[c0:c0 + ff_tile, :])
    if final:
        acc = _rmsnorm(acc, fnw_ref[...])
    o_ref[...] = acc


def _proj_mlp(x, a, wo, nw, wup, wdn, fnw, final, tm=1024):
    t, d = x.shape
    d_ff = wup.shape[1]
    const = lambda i: (0, 0)
    once = pl.Buffered(1)
    return pl.pallas_call(
        functools.partial(_proj_mlp_kernel, final=final, ff_tile=min(1024, d_ff)),
        grid=(t // tm,),
        in_specs=[pl.BlockSpec((tm, d), lambda i: (i, 0)),
                  pl.BlockSpec((tm, a.shape[1]), lambda i: (i, 0)),
                  pl.BlockSpec(wo.shape, const, pipeline_mode=once),
                  pl.BlockSpec((1, d), const),
                  pl.BlockSpec(wup.shape, const, pipeline_mode=once),
                  pl.BlockSpec(wdn.shape, const, pipeline_mode=once),
                  pl.BlockSpec((1, d), const)],
        out_specs=pl.BlockSpec((tm, d), lambda i: (i, 0)),
        out_shape=jax.ShapeDtypeStruct((t, d), F32),
        compiler_params=_params(1),
        name="proj_mlp",
    )(x, a, wo, nw.reshape(1, d), wup, wdn, fnw.reshape(1, d))


def _kv_proj_kernel(x_ref, nw_ref, wc_ref, wk_ref, wvt_ref, kc_ref, vc_ref, ks_ref, kw_ref, vst_ref, vwt_ref):
    i = pl.program_id(0)
    tm = x_ref.shape[0]
    gd = NSA_KV_HEADS * NSA_DH
    h = _rmsnorm(x_ref[...], nw_ref[...]).astype(BF16)
    cmp_in = _dot(h, wc_ref[...])
    for g in range(NSA_KV_HEADS):
        kc_ref[g] = cmp_in[:, g * NSA_DH:(g + 1) * NSA_DH]
        vc_ref[g] = cmp_in[:, gd + g * NSA_DH:gd + (g + 1) * NSA_DH]
    kk = _dot(h, wk_ref[...])
    row = lax.broadcasted_iota(jnp.int32, (tm, AUG), 0)
    lane = lax.broadcasted_iota(jnp.int32, (tm, AUG), 1)
    tok = i * tm + row
    tile = (tok // KEY_TILE).astype(F32)
    offs = (tok % KEY_TILE).astype(F32)
    consts = jnp.where((lane >= NSA_DH) & (lane < NSA_DH + 3), tile,
                       jnp.where((lane >= NSA_DH + 3) & (lane < NSA_DH + 6), 1.0,
                                 jnp.where((lane >= NSA_DH + 6) & (lane < NSA_DH + 9), offs, 0.0)))
    for g in range(NSA_KV_HEADS):
        ks_ref[g] = (kk[:, g * AUG:(g + 1) * AUG] + consts).astype(BF16)
        kw_ref[g] = (kk[:, (NSA_KV_HEADS + g) * AUG:(NSA_KV_HEADS + g + 1) * AUG] + consts).astype(BF16)
    vt = _nt(wvt_ref[...], h)
    r16 = lax.broadcasted_iota(jnp.int32, (VROWS - NSA_DH, tm), 0)
    ones_blk = jnp.where(r16 == 0, 1.0, 0.0).astype(BF16)
    for g in range(NSA_KV_HEADS):
        vst_ref[g, 0:NSA_DH, :] = vt[g * NSA_DH:(g + 1) * NSA_DH, :].astype(BF16)
        vst_ref[g, NSA_DH:VROWS, :] = ones_blk
        vwt_ref[g, 0:NSA_DH, :] = vt[gd + g * NSA_DH:gd + (g + 1) * NSA_DH, :].astype(BF16)
        vwt_ref[g, NSA_DH:VROWS, :] = ones_blk


def _kv_proj(x, nw, wc, wk, wvt, tm=256):
    t, d = x.shape
    g = NSA_KV_HEADS
    const = lambda i: (0, 0)
    rows = lambda i: (0, i, 0)
    cols = lambda i: (0, 0, i)
    return pl.pallas_call(
        _kv_proj_kernel,
        grid=(t // tm,),
        in_specs=[pl.BlockSpec((tm, d), lambda i: (i, 0)),
                  pl.BlockSpec((1, d), const),
                  pl.BlockSpec(wc.shape, const),
                  pl.BlockSpec(wk.shape, const),
                  pl.BlockSpec(wvt.shape, const)],
        out_specs=[pl.BlockSpec((g, tm, NSA_DH), rows), pl.BlockSpec((g, tm, NSA_DH), rows),
                   pl.BlockSpec((g, tm, AUG), rows), pl.BlockSpec((g, tm, AUG), rows),
                   pl.BlockSpec((g, VROWS, tm), cols), pl.BlockSpec((g, VROWS, tm), cols)],
        out_shape=[jax.ShapeDtypeStruct((g, t, NSA_DH), F32), jax.ShapeDtypeStruct((g, t, NSA_DH), F32),
                   jax.ShapeDtypeStruct((g, t, AUG), BF16), jax.ShapeDtypeStruct((g, t, AUG), BF16),
                   jax.ShapeDtypeStruct((g, VROWS, t), BF16), jax.ShapeDtypeStruct((g, VROWS, t), BF16)],
        compiler_params=_params(1),
        name="kv_proj",
    )(x, nw.reshape(1, d), wc, wk, wvt)


def _nsa_inproj_kernel(x_ref, nw_ref, wt_ref, sl_ref, q_ref, gl_ref, *, nqb):
    i = pl.program_id(0)
    qw = NSA_HEADS * NSA_DH
    h = _rmsnorm(x_ref[...], nw_ref[...]).astype(BF16)
    pt = _nt(wt_ref[...], h)
    r16 = lax.broadcasted_iota(jnp.int32, (16, QL), 0)
    ql = (lax.broadcasted_iota(jnp.int32, (1, QL), 1) & (Q_BLOCK - 1)).astype(F32)

    def split3(v):
        a = v.astype(BF16).astype(F32)
        b = (v - a).astype(BF16).astype(F32)
        return a, b, v - a - b

    for g in range(NSA_KV_HEADS):
        slope = sl_ref[g]
        s1, s2, s3 = split3(slope * float(KEY_TILE))
        r1, r2, r3 = split3(slope)
        for qb in range(nqb):
            lanes = slice(qb * Q_BLOCK, (qb + 1) * Q_BLOCK)
            for j in range(NSA_GROUP):
                r0 = (g * NSA_GROUP + j) * NSA_DH
                q_ref[g, qb, 0:NSA_DH, j * Q_BLOCK:(j + 1) * Q_BLOCK] = pt[r0:r0 + NSA_DH, lanes].astype(BF16)
            tq = ((i * nqb + qb) * Q_BLOCK).astype(F32) + ql
            o1, o2, o3 = split3(-(slope * tq))
            aug = jnp.zeros((16, QL), F32)
            for k, term in enumerate((s1, s2, s3, o1, o2, o3, r1, r2, r3)):
                aug = jnp.where(r16 == k, term, aug)
            q_ref[g, qb, NSA_DH:NSA_DH + 16, :] = aug.astype(BF16)
            q_ref[g, qb, NSA_DH + 16:AUG, :] = jnp.zeros((AUG - NSA_DH - 16, QL), BF16)
            for c in range(3):
                for j in range(NSA_GROUP):
                    r0 = qw + (g * 3 + c) * NSA_GROUP + j
                    gl_ref[g, qb, c:c + 1, j * Q_BLOCK:(j + 1) * Q_BLOCK] = pt[r0:r0 + 1, lanes]


def _nsa_inproj(x, nw, wt, slopes, nqb=2):
    t, d = x.shape
    nq = t // Q_BLOCK
    g = NSA_KV_HEADS
    tm = nqb * Q_BLOCK
    return pl.pallas_call(
        functools.partial(_nsa_inproj_kernel, nqb=nqb),
        grid=(t // tm,),
        in_specs=[pl.BlockSpec((tm, d), lambda i: (i, 0)),
                  pl.BlockSpec((1, d), lambda i: (0, 0)),
                  pl.BlockSpec(wt.shape, lambda i: (0, 0)),
                  pl.BlockSpec(slopes.shape, lambda i: (0, 0, 0))],
        out_specs=[pl.BlockSpec((g, nqb, AUG, QL), lambda i: (0, i, 0, 0)),
                   pl.BlockSpec((g, nqb, 3, QL), lambda i: (0, i, 0, 0))],
        out_shape=[jax.ShapeDtypeStruct((g, nq, AUG, QL), BF16),
                   jax.ShapeDtypeStruct((g, nq, 3, QL), F32)],
        compiler_params=_params(1),
        name="nsa_inproj",
    )(x, nw.reshape(1, d), wt, slopes)


def _compress_kernel(hf_ref, pe_ref, w1_ref, w2_ref, o_ref):
    x = hf_ref[0, 0]
    nc, half = x.shape
    pe_lo = pe_ref[0, 0:1, :]
    pe_hi = pe_ref[0, 1:2, :]
    w1_lo = w1_ref[0, 0:half, :]
    w1_hi = w1_ref[0, half:2 * half, :]
    u = _dot((x + pe_lo).astype(BF16), w1_lo)
    vv = _dot((x + pe_hi).astype(BF16), w1_hi)
    nxt = pltpu.roll(vv, nc - 1, 0)
    pad = _dot(jnp.broadcast_to(pe_hi, (8, half)).astype(BF16), w1_hi)[0:1]
    row = lax.broadcasted_iota(jnp.int32, vv.shape, 0)
    pre = u + jnp.where(row == nc - 1, pad, nxt)
    hid = 0.5 * pre * (1.0 + jnp.tanh(0.7978845608028654 * (pre + 0.044715 * (pre * pre * pre))))
    o_ref[0, 0] = _dot(hid.astype(BF16), w2_ref[0])


def _compress(hf, pe, w1, w2):
    s, g, nc, half = hf.shape
    return pl.pallas_call(
        _compress_kernel,
        grid=(s, g),
        in_specs=[pl.BlockSpec((1, 1, nc, half), lambda i, j: (i, j, 0, 0)),
                  pl.BlockSpec((1, 2, half), lambda i, j: (i, 0, 0)),
                  pl.BlockSpec((1,) + w1.shape[1:], lambda i, j: (i, 0, 0)),
                  pl.BlockSpec((1,) + w2.shape[1:], lambda i, j: (i, 0, 0))],
        out_specs=pl.BlockSpec((1, 1, nc, w2.shape[2]), lambda i, j: (i, j, 0, 0)),
        out_shape=jax.ShapeDtypeStruct((s, g, nc, w2.shape[2]), F32),
        compiler_params=_params(2),
        name="compress",
    )(hf, pe, w1, w2)


def _nsa_cmp_kernel(q_ref, kc_ref, vct_ref, cb_ref, oc_ref, sel_ref, l_ref, p_ref, *, ns, topk, nqb):
    ratio = SLC_BLOCK // CMP_STRIDE
    qis = [pl.program_id(1) * nqb + qb for qb in range(nqb)]

    def attend(nv):
        cb = cb_ref[0:nv, :]
        n_i = lax.broadcasted_iota(jnp.int32, (nv, Q_BLOCK), 0)
        ql = lax.broadcasted_iota(jnp.int32, (nv, Q_BLOCK), 1)
        n_f = n_i.astype(F32)
        mxs = []
        for qb in range(nqb):
            qa = q_ref[0, qb]
            t0f = (qis[qb] * Q_BLOCK).astype(F32)
            mx = jnp.full((1, QL), NEG_INF, F32)
            for u in range(ratio):
                s = _dot(kc_ref[0, u * ns:u * ns + nv, :], qa)
                s = jnp.where(cb >= float(CMP_STRIDE * u + CMP_BLOCK - 1) - t0f, s, NEG_INF)
                p_ref[qb, u * ns:u * ns + nv, :] = s
                mx = jnp.maximum(mx, jnp.max(s, axis=0, keepdims=True))
            mxs.append(mx)
        cands, forceds = [], []
        for qb in range(nqb):
            mx = mxs[qb]
            valid = mx > 0.5 * NEG_INF
            acc = jnp.zeros((VROWS, QL), F32)
            head_pool = [None] * NSA_GROUP
            for u in range(ratio):
                p = jnp.exp(p_ref[qb, u * ns:u * ns + nv, :] - mx)
                acc = acc + _dot(vct_ref[0, :, u * ns:u * ns + nv], p.astype(BF16))
                for j in range(NSA_GROUP):
                    pj = p[:, j * Q_BLOCK:(j + 1) * Q_BLOCK]
                    head_pool[j] = pj if head_pool[j] is None else head_pool[j] + pj
                    if u == ratio - 1:
                        head_pool[j] = head_pool[j] + jnp.where(n_i >= 1, pltpu.roll(pj, 1, 0), 0.0)
            inv = jnp.where(valid, 1.0 / acc[NSA_DH:NSA_DH + 1], 0.0)
            oc_ref[0, qb] = acc[0:NSA_DH] * inv
            pooled = head_pool[0] * inv[:, 0:Q_BLOCK]
            for j in range(1, NSA_GROUP):
                pooled = pooled + head_pool[j] * inv[:, j * Q_BLOCK:(j + 1) * Q_BLOCK]
            cur = 2 * qis[qb] + (ql >= SLC_BLOCK).astype(jnp.int32)
            ok = n_i <= cur
            forced = ok & ((n_i == 0) | (n_i == cur) | (n_i == cur - 1))
            forceds.append(forced)
            cands.append(jnp.where(ok & jnp.logical_not(forced), pooled, NEG_INF))
        scores = list(cands)
        for _ in range(topk - 3):
            for qb in range(nqb):
                best = jnp.max(scores[qb], axis=0, keepdims=True)
                idx = jnp.min(jnp.where(scores[qb] == best, n_f, float(ns)), axis=0, keepdims=True)
                scores[qb] = jnp.where(n_f == idx, -jnp.inf, scores[qb])
        nt = ns // 2
        pair = jnp.where((lax.broadcasted_iota(jnp.int32, (nv, 128), 0) >> 1)
                         == lax.broadcasted_iota(jnp.int32, (nv, 128), 1), 1.0, 0.0)
        lane = lax.broadcasted_iota(jnp.int32, (1, 128), 1)
        before = (lax.broadcasted_iota(jnp.int32, (128, 128), 0)
                  < lax.broadcasted_iota(jnp.int32, (128, 128), 1)).astype(BF16)
        k_f = lax.broadcasted_iota(jnp.int32, (nt, 128), 0).astype(F32)
        for qb in range(nqb):
            picked = (scores[qb] == -jnp.inf) & (cands[qb] > 0.5 * NEG_INF)
            sel = jnp.where(forceds[qb] | picked, 1.0, 0.0)
            sel_ref[0, qb, 0:nv, :] = sel
            if nv < ns:
                sel_ref[0, qb, nv:ns, :] = jnp.zeros((ns - nv, Q_BLOCK), F32)
            any_q = jnp.max(sel, axis=1, keepdims=True)
            active = (jnp.sum(any_q * pair, axis=0, keepdims=True) > 0.5) & (lane < qis[qb])
            act = jnp.where(active, 1.0, 0.0)
            slot = _dot(jnp.broadcast_to(act, (8, 128)).astype(BF16), before)[0:1]
            hit = jnp.where((k_f == slot) & active, lane.astype(F32), 0.0)
            lst = jnp.sum(hit, axis=1, keepdims=True)
            cnt = jnp.sum(act, axis=1, keepdims=True)
            lst = jnp.where(lax.broadcasted_iota(jnp.int32, (nt, 1), 0) == nt - 1, cnt, lst)
            l_ref[0, qb] = lst.astype(jnp.int32)

    n_cls = 8
    step = ns // n_cls
    need = 2 * qis[-1] + 2
    for k in range(n_cls):
        lo, hi = k * step, (k + 1) * step

        @pl.when((need > lo) & (need <= hi))
        def _():
            attend(hi)


def _nsa_cmp(qa, kc, vct, cbase):
    g, nq, _, _ = qa.shape
    dh = NSA_DH
    ncmp = kc.shape[1]
    ns = ncmp // (SLC_BLOCK // CMP_STRIDE)
    topk = min(SLC_TOPK, ns)
    nqb = CMP_QBLOCKS
    return pl.pallas_call(
        functools.partial(_nsa_cmp_kernel, ns=ns, topk=topk, nqb=nqb),
        grid=(g, nq // nqb),
        in_specs=[pl.BlockSpec((1, nqb, AUG, QL), lambda i, j: (i, j, 0, 0)),
                  pl.BlockSpec((1, ncmp, AUG), lambda i, j: (i, 0, 0)),
                  pl.BlockSpec((1, VROWS, ncmp), lambda i, j: (i, 0, 0)),
                  pl.BlockSpec((ns, QL), lambda i, j: (0, 0))],
        out_specs=[pl.BlockSpec((1, nqb, dh, QL), lambda i, j: (i, j, 0, 0)),
                   pl.BlockSpec((1, nqb, ns, Q_BLOCK), lambda i, j: (i, j, 0, 0)),
                   pl.BlockSpec((1, nqb, ns // 2, 1), lambda i, j: (i, j, 0, 0))],
        out_shape=[jax.ShapeDtypeStruct((g, nq, dh, QL), F32),
                   jax.ShapeDtypeStruct((g, nq, ns, Q_BLOCK), F32),
                   jax.ShapeDtypeStruct((g, nq, ns // 2, 1), jnp.int32)],
        scratch_shapes=[pltpu.VMEM((nqb, ncmp, QL), F32)],
        compiler_params=_params(2),
        name="nsa_cmp",
    )(qa, kc, vct, cbase)


def _nsa_sw_kernel(lists_ref, q_ref, gl_ref, oc_ref, sel_ref, base_ref, ks_ref, vst_ref,
                   kw_ref, vwt_ref, o_ref, m_ref, acc_ref, *, nq, nqb):
    g = pl.program_id(0)
    step = pl.program_id(1)
    half = SLC_BLOCK
    tile_base = base_ref[...]

    def probs(s, mx):
        return jnp.exp((s - mx).astype(BF16))

    def sel_mask(ra, rb):
        mk = jnp.concatenate([jnp.broadcast_to(ra, (half, Q_BLOCK)), jnp.broadcast_to(rb, (half, Q_BLOCK))], axis=0)
        return jnp.concatenate([mk] * NSA_GROUP, axis=1) > 0.5

    def listed_tile(qb, cnt, slot):
        live = slot < cnt
        m = jnp.where(live, lists_ref[(g * nq + qis[qb]) * nq + jnp.minimum(slot, nq - 2)], 0)
        keep = jnp.where(live, 1.0, 0.0)
        pos = pl.multiple_of(m * KEY_TILE, KEY_TILE)
        mask = sel_mask(sel_ref[0, qb, pl.ds(2 * m, 1), :] * keep, sel_ref[0, qb, pl.ds(2 * m + 1, 1), :] * keep)
        return ks_ref[0, pl.ds(pos, KEY_TILE), :], vst_ref[0, :, pl.ds(pos, KEY_TILE)], mask

    def masked_scores(qa, k_tiles, masks):
        sc = _dot(jnp.concatenate(k_tiles, axis=0), qa)
        tiles, mx = [], None
        for u, mask in enumerate(masks):
            s = jnp.where(mask, sc[u * KEY_TILE:(u + 1) * KEY_TILE, :], NEG_INF)
            mu = jnp.max(s, axis=0, keepdims=True)
            mx = mu if mx is None else jnp.maximum(mx, mu)
            tiles.append(s)
        return tiles, mx

    def pv(s_tiles, v_tiles, mx):
        return _dot(jnp.concatenate(v_tiles, axis=1), jnp.concatenate([probs(s, mx) for s in s_tiles], axis=0))

    qis = [step * nqb + qb for qb in range(nqb)]
    cnts = [lists_ref[(g * nq + qi) * nq + nq - 1] for qi in qis]

    n_wt = (WINDOW + Q_BLOCK) // KEY_TILE
    staged = []
    for qb in range(nqb):
        qi, cnt = qis[qb], cnts[qb]
        t0 = qi * Q_BLOCK
        qa = q_ref[0, qb]
        wk, wv, wmask = [], [], []
        for u in range(n_wt):
            pos_raw = t0 - WINDOW + u * KEY_TILE
            pos = pl.multiple_of(jnp.maximum(pos_raw, 0), KEY_TILE)
            wk.append(kw_ref[0, pl.ds(pos, KEY_TILE), :])
            wv.append(vwt_ref[0, :, pl.ds(pos, KEY_TILE)])
            live = pos_raw >= 0
            if u == 0:
                wmask.append(jnp.logical_and(tile_base < 0, live))
            elif u == n_wt - 1:
                wmask.append(tile_base >= 0)
            else:
                wmask.append(live)
        posd = pl.multiple_of(t0, KEY_TILE)
        diag_mask = sel_mask(sel_ref[0, qb, pl.ds(2 * qi, 1), :],
                             sel_ref[0, qb, pl.ds(2 * qi + 1, 1), :]) & (tile_base >= 0)
        listed = [listed_tile(qb, cnt, jnp.int32(u)) for u in range(SEL_STATIC)]
        stiles, m0 = masked_scores(qa, [ks_ref[0, pl.ds(posd, KEY_TILE), :]] + [t[0] for t in listed],
                                   [diag_mask] + [t[2] for t in listed])
        wtiles, mw = masked_scores(qa, wk, wmask)
        staged.append((stiles, m0, [vst_ref[0, :, pl.ds(posd, KEY_TILE)]] + [t[1] for t in listed], wtiles, mw, wv))

    o_wins = []
    for qb in range(nqb):
        stiles, m0, svals, wtiles, mw, wv = staged[qb]
        m_ref[qb] = m0
        acc_ref[qb] = pv(stiles, svals, m0)
        accw = pv(wtiles, wv, mw)
        o_wins.append(accw[0:NSA_DH] * (1.0 / accw[NSA_DH:NSA_DH + 1]))

    def chunk_body(ci, carry):
        staged = []
        for qb in range(nqb):
            listed = [listed_tile(qb, cnts[qb], SEL_STATIC + ci * SEL_CHUNK + u) for u in range(SEL_CHUNK)]
            tiles, cmax = masked_scores(q_ref[0, qb], [t[0] for t in listed], [t[2] for t in listed])
            staged.append((tiles, cmax, [t[1] for t in listed]))
        for qb in range(nqb):
            tiles, cmax, vals = staged[qb]
            m_old = m_ref[qb]
            m_new = jnp.maximum(m_old, cmax)
            acc_ref[qb] = jnp.exp(m_old - m_new) * acc_ref[qb] + pv(tiles, vals, m_new)
            m_ref[qb] = m_new
        return carry

    most = cnts[0]
    for cnt in cnts[1:]:
        most = jnp.maximum(most, cnt)
    lax.fori_loop(0, (jnp.maximum(most - SEL_STATIC, 0) + SEL_CHUNK - 1) // SEL_CHUNK, chunk_body, 0)

    for qb in range(nqb):
        o_sel = acc_ref[qb, 0:NSA_DH, :] * (1.0 / acc_ref[qb, NSA_DH:NSA_DH + 1, :])

        gl = gl_ref[0, qb]
        gate = 1.0 / (1.0 + jnp.exp(-gl))
        out_t = gate[0:1] * oc_ref[0, qb] + gate[1:2] * o_sel + gate[2:3] * o_wins[qb]
        halves = []
        for j in range(0, NSA_GROUP, 2):
            pair = jnp.concatenate([out_t[:, j * Q_BLOCK:(j + 1) * Q_BLOCK],
                                    out_t[:, (j + 1) * Q_BLOCK:(j + 2) * Q_BLOCK]], axis=0)
            halves.append(pair.T)
        o_ref[qb * Q_BLOCK:(qb + 1) * Q_BLOCK, :] = jnp.concatenate(halves, axis=1).astype(o_ref.dtype)


def _nsa_sw(lists, qa, glog, oct_, selt, base, ks, vst, kw, vwt):
    g, nq, _, _ = qa.shape
    dh = NSA_DH
    t = ks.shape[1]
    ns = selt.shape[2]
    assert lists.shape[0] == g * nq * nq
    nqb = SW_QBLOCKS
    grid_spec = pltpu.PrefetchScalarGridSpec(
        num_scalar_prefetch=1,
        grid=(g, nq // nqb),
        in_specs=[pl.BlockSpec((1, nqb, AUG, QL), lambda i, j, w: (i, j, 0, 0)),
                  pl.BlockSpec((1, nqb, 3, QL), lambda i, j, w: (i, j, 0, 0)),
                  pl.BlockSpec((1, nqb, dh, QL), lambda i, j, w: (i, j, 0, 0)),
                  pl.BlockSpec((1, nqb, ns, Q_BLOCK), lambda i, j, w: (i, j, 0, 0)),
                  pl.BlockSpec((KEY_TILE, QL), lambda i, j, w: (0, 0)),
                  pl.BlockSpec((1, t, AUG), lambda i, j, w: (i, 0, 0)),
                  pl.BlockSpec((1, VROWS, t), lambda i, j, w: (i, 0, 0)),
                  pl.BlockSpec((1, t, AUG), lambda i, j, w: (i, 0, 0)),
                  pl.BlockSpec((1, VROWS, t), lambda i, j, w: (i, 0, 0))],
        out_specs=pl.BlockSpec((nqb * Q_BLOCK, NSA_GROUP * dh), lambda i, j, w: (j, i)),
        scratch_shapes=[pltpu.VMEM((nqb, 1, QL), F32), pltpu.VMEM((nqb, VROWS, QL), F32)],
    )
    return pl.pallas_call(
        functools.partial(_nsa_sw_kernel, nq=nq, nqb=nqb),
        grid_spec=grid_spec,
        out_shape=jax.ShapeDtypeStruct((t, g * NSA_GROUP * dh), BF16),
        compiler_params=_params(2),
        name="nsa_sw",
    )(lists, qa, glog, oct_, selt, base, ks, vst, kw, vwt)


def kernel(x, a_norm_w, a_w_in, a_gnorm_w, a_w_out, a_lower_bounds, kv_norm_w, kv_w, cmp_pe_k, cmp_w1_k,
           cmp_w2_k, cmp_pe_v, cmp_w1_v, cmp_w2_v, b_norm_w, b_w_in, b_w_out, mlp_norm_w, mlp_w_up,
           mlp_w_down, final_norm_w):
    bsz, t, d = x.shape
    assert bsz == 1 and t % Q_BLOCK == 0 and t >= WINDOW + Q_BLOCK
    n_a = a_w_in.shape[0]
    n_b = b_w_in.shape[0]
    nq = t // Q_BLOCK
    xs = x[0]

    for layer in range(n_a):
        o = _hgrn(xs, a_norm_w[layer], a_w_in[layer].astype(BF16), a_lower_bounds, a_gnorm_w[layer], layer)
        xs = _proj_mlp(xs, o, a_w_out[layer].astype(BF16), mlp_norm_w[layer],
                       mlp_w_up[layer].astype(BF16), mlp_w_down[layer].astype(BF16),
                       final_norm_w, final=False)

    gd = NSA_KV_HEADS * NSA_DH
    kvw = kv_w.reshape(d, N_KV_STREAMS, NSA_KV_HEADS, NSA_DH)
    wc = kvw[:, 0:2].reshape(d, 2 * gd).astype(BF16)
    wk = jnp.pad(jnp.stack([kvw[:, 2], kvw[:, 4]], axis=1), ((0, 0), (0, 0), (0, 0), (0, AUG - NSA_DH)))
    wk = wk.reshape(d, 2 * NSA_KV_HEADS * AUG).astype(BF16)
    wvt = jnp.stack([kvw[:, 3], kvw[:, 5]], axis=1).reshape(d, 2 * gd).T.astype(BF16)
    kc_in, vc_in, ks, kw, vst, vwt = _kv_proj(xs, kv_norm_w, wc, wk, wvt)

    ncmp = t // CMP_STRIDE
    half = CMP_STRIDE * NSA_DH
    hf = jnp.stack([kc_in.reshape(NSA_KV_HEADS, ncmp, half), vc_in.reshape(NSA_KV_HEADS, ncmp, half)])
    pe = jnp.stack([cmp_pe_k.reshape(2, half), cmp_pe_v.reshape(2, half)])
    w1 = jnp.stack([cmp_w1_k, cmp_w1_v]).astype(BF16)
    w2 = jnp.stack([cmp_w2_k, cmp_w2_v]).astype(BF16)
    cmp_out = _compress(hf, pe, w1, w2)
    ratio = SLC_BLOCK // CMP_STRIDE
    ns = ncmp // ratio
    perm = cmp_out.reshape(2, NSA_KV_HEADS, ns, ratio, NSA_DH).transpose(0, 1, 3, 2, 4).reshape(
        2, NSA_KV_HEADS, ncmp, NSA_DH)
    crow = jnp.arange(ncmp)
    cmp_end = SLC_BLOCK * (crow % ns) + CMP_STRIDE * (crow // ns) + (CMP_BLOCK - 1)
    aug_cols = jnp.stack([cmp_end // KEY_TILE] * 3 + [jnp.ones_like(cmp_end)] * 3 + [cmp_end % KEY_TILE] * 3,
                         axis=1).astype(F32)
    aug_cols = jnp.pad(aug_cols, ((0, 0), (0, AUG - NSA_DH - aug_cols.shape[1])))
    kc = jnp.concatenate([perm[0], jnp.broadcast_to(aug_cols, (NSA_KV_HEADS, ncmp, AUG - NSA_DH))],
                         axis=2).astype(BF16)
    ones_rows = jnp.zeros((NSA_KV_HEADS, VROWS - NSA_DH, ncmp), F32).at[:, 0, :].set(1.0)
    vct = jnp.concatenate([perm[1].transpose(0, 2, 1), ones_rows], axis=1).astype(BF16)

    slopes = jnp.exp2(-8.0 * jnp.arange(1, NSA_HEADS + 1, dtype=F32) / NSA_HEADS)
    slopes = jnp.repeat(slopes.reshape(NSA_KV_HEADS, 1, NSA_GROUP), Q_BLOCK, axis=2)
    q_off = jnp.tile(jnp.arange(Q_BLOCK, dtype=F32), NSA_GROUP)[None, :]
    base = q_off - jnp.arange(KEY_TILE, dtype=F32)[:, None]
    cbase = q_off - float(SLC_BLOCK) * jnp.arange(ns, dtype=F32)[:, None]

    qw = NSA_HEADS * NSA_DH
    n_gate = b_w_in.shape[2] - qw
    pad_rows = (-(qw + n_gate)) % 128
    for b in range(n_b):
        w_q = b_w_in[b][:, 0:qw] * (NSA_DH ** -0.5)
        w_g = b_w_in[b][:, qw:].reshape(d, NSA_KV_HEADS, NSA_GROUP, 3).transpose(0, 1, 3, 2).reshape(d, n_gate)
        wt_in = jnp.pad(jnp.concatenate([w_q, w_g], axis=1), ((0, 0), (0, pad_rows))).T.astype(BF16)
        qa, glog = _nsa_inproj(xs, b_norm_w[b], wt_in, slopes)
        oct_, selt, lists = _nsa_cmp(qa, kc, vct, cbase)
        o = _nsa_sw(lists.reshape(-1), qa, glog, oct_, selt, base, ks, vst, kw, vwt)
        xs = _proj_mlp(xs, o, b_w_out[b].astype(BF16), mlp_norm_w[n_a + b],
                       mlp_w_up[n_a + b].astype(BF16), mlp_w_down[n_a + b].astype(BF16),
                       final_norm_w, final=(b == n_b - 1))
    return xs[None]
```

```python
import functools

import jax
import jax.numpy as jnp
from jax import lax
from jax.experimental import pallas as pl
from jax.experimental.pallas import tpu as pltpu

F32 = jnp.float32
BF16 = jnp.bfloat16

NORM_EPS = 1e-6
NEG_INF = -1e30
GATE_FLOOR = 1e-30

HG_HEADS = 8
HG_DK = 128
HG_DV = 128
HG_CHUNK = 64
HG_STEP_CHUNKS = 8

NSA_HEADS = 16
NSA_KV_HEADS = 4
NSA_GROUP = NSA_HEADS // NSA_KV_HEADS
NSA_DH = 64
CMP_BLOCK = 32
CMP_STRIDE = 16
SLC_BLOCK = 64
SLC_TOPK = 16
WINDOW = 512
Q_BLOCK = 128
FORCE_BONUS = 1e4
N_KV_STREAMS = 6

KEY_TILE = 128
SEL_STATIC = 8
SEL_CHUNK = 2
SW_QBLOCKS = 4
CMP_QBLOCKS = 4
QL = NSA_GROUP * Q_BLOCK
AUG = 128
VROWS = NSA_DH + 16

VMEM_LIMIT = 56 * 1024 * 1024


def _params(n_axes, vmem=VMEM_LIMIT):
    return pltpu.CompilerParams(dimension_semantics=("arbitrary",) * n_axes, vmem_limit_bytes=vmem)


def _nt(a, b):
    return lax.dot_general(a, b, (((1,), (1,)), ((), ())), preferred_element_type=F32)


def _tn(a, b):
    return lax.dot_general(a, b, (((0,), (0,)), ((), ())), preferred_element_type=F32)


def _dot(a, b):
    return jnp.dot(a, b, preferred_element_type=F32)


def _rmsnorm(x, w):
    ms = jnp.mean(x * x, axis=-1, keepdims=True)
    return x * lax.rsqrt(ms + NORM_EPS) * w


def _hgrn_kernel(x_ref, nw_ref, w_ref, lbp_ref, gw_ref, seg_ref, o_ref, st_ref, *, layer, n_chunks):
    c = HG_CHUNK
    hk = HG_HEADS * HG_DK

    @pl.when(pl.program_id(0) == 0)
    def _():
        st_ref[...] = jnp.zeros_like(st_ref)

    hn = _rmsnorm(x_ref[...], nw_ref[...]).astype(BF16)
    proj = {grp: _dot(hn, w_ref[:, grp * hk:(grp + 1) * hk]) for grp in (1, 0, 2, 3)}

    a = lbp_ref[...]
    e = jnp.exp(a - jnp.max(a, axis=0, keepdims=True))
    p = e / jnp.sum(e, axis=0, keepdims=True)
    cum = p[0:1]
    for i in range(1, layer + 1):
        cum = cum + p[i:i + 1]
    lb = cum - p[0:1]
    one_m = 1.0 - lb

    sub = lax.broadcasted_iota(jnp.int32, (8, hk), 0)
    ti = lax.broadcasted_iota(jnp.int32, (c, c), 0)
    si = lax.broadcasted_iota(jnp.int32, (c, c), 1)
    masks = [(((ti >> j) & 1) == 1) & (((si >> j) & 1) == 0) & ((ti >> (j + 1)) == (si >> (j + 1)))
             for j in range(6)]
    diag = ti == si
    gw = gw_ref[...]

    for ch in range(n_chunks):
        rows = slice(ch * c, (ch + 1) * c)
        q = proj[0][rows]
        fp = proj[1][rows]
        v = proj[2][rows]
        go = proj[3][rows]

        ea = jnp.exp(-jnp.abs(fp))
        r = 1.0 / (1.0 + ea)
        pos = fp >= 0
        f_gate = lb + one_m * jnp.where(pos, r, ea * r)
        kk = one_m * jnp.where(pos, ea * r, r)
        logf = jnp.log(jnp.maximum(f_gate, GATE_FLOOR))

        hi = logf.astype(BF16)
        rem = logf - hi.astype(F32)
        mid = rem.astype(BF16)
        lo = (rem - mid.astype(F32)).astype(BF16)
        sums = _dot(seg_ref[...], jnp.concatenate([hi, mid, lo], axis=0))
        b = sums[0:c]
        eb = jnp.exp(b[c - 1:c, :])

        xs = []
        for j in range(6):
            ex = jnp.exp(sums[c * (j + 1):c * (j + 2)])
            slabs = []
            for r0 in range(0, c, 8):
                if j < 3:
                    slabs.append(jnp.where(((sub >> j) & 1) == 1, q[r0:r0 + 8], kk[r0:r0 + 8]))
                else:
                    slabs.append(q[r0:r0 + 8] if (r0 >> j) & 1 else kk[r0:r0 + 8])
            xs.append((jnp.concatenate(slabs, axis=0) * ex).astype(BF16))

        vb = v.astype(BF16)
        qs = (q * jnp.exp(b)).astype(BF16)
        khat = (kk * jnp.exp(sums[7 * c:8 * c])).astype(BF16)

        heads = [slice(h * HG_DK, (h + 1) * HG_DK) for h in range(HG_HEADS)]
        qk = q * kk
        attn = [jnp.where(diag, jnp.sum(qk[:, sl], axis=-1, keepdims=True), 0.0) for sl in heads]
        for j in range(6):
            for h, sl in enumerate(heads):
                xj = xs[j][:, sl]
                attn[h] = attn[h] + jnp.where(masks[j], _nt(xj, xj), 0.0)
        outs = []
        for h, sl in enumerate(heads):
            st = st_ref[h]
            outs.append(_dot(attn[h].astype(BF16), vb[:, sl]) + _nt(qs[:, sl], st.astype(BF16)))
            st_ref[h] = st * eb[:, sl] + _tn(vb[:, sl], khat[:, sl])
        for h, sl in enumerate(heads):
            g = go[:, sl]
            o_ref[rows, sl] = (_rmsnorm(outs[h], gw) * (g * (1.0 / (1.0 + jnp.exp(-g))))).astype(o_ref.dtype)


def _segment_matrix():
    c = HG_CHUNK
    t = jnp.arange(c)[:, None]
    i = jnp.arange(c)[None, :]
    parts = [i <= t]
    for j in range(6):
        m = 1 << j
        ref = t - (t % (2 * m)) + m
        parts.append(jnp.where(t >= ref, (i > ref) & (i <= t), (i > t) & (i <= ref)))
    parts.append(i > t)
    return jnp.tile(jnp.concatenate(parts, axis=0), (1, 3)).astype(BF16)


def _hgrn(x, norm_w, w_in, lb_param, gnorm_w, layer):
    t, d = x.shape
    hk = HG_HEADS * HG_DK
    hv = HG_HEADS * HG_DV
    n_layers = lb_param.shape[0]
    seg = _segment_matrix()
    tm = HG_STEP_CHUNKS * HG_CHUNK
    const = lambda i: (0, 0)
    return pl.pallas_call(
        functools.partial(_hgrn_kernel, layer=layer, n_chunks=HG_STEP_CHUNKS),
        grid=(t // tm,),
        in_specs=[pl.BlockSpec((tm, d), lambda i: (i, 0)),
                  pl.BlockSpec((1, d), const),
                  pl.BlockSpec(w_in.shape, const),
                  pl.BlockSpec((n_layers, hk), const),
                  pl.BlockSpec((1, HG_DV), const),
                  pl.BlockSpec(seg.shape, const)],
        out_specs=pl.BlockSpec((tm, hv), lambda i: (i, 0)),
        out_shape=jax.ShapeDtypeStruct((t, hv), BF16),
        scratch_shapes=[pltpu.VMEM((HG_HEADS, HG_DV, HG_DK), F32)],
        compiler_params=_params(1),
        name="hgrn",
    )(x, norm_w.reshape(1, d), w_in, lb_param, gnorm_w.reshape(1, HG_DV), seg)


def _proj_mlp_kernel(x_ref, a_ref, wo_ref, nw_ref, wup_ref, wdn_ref, fnw_ref, o_ref, *, final, ff_tile):
    x1 = x_ref[...] + _dot(a_ref[...], wo_ref[...])
    h = _rmsnorm(x1, nw_ref[...]).astype(BF16)
    acc = x1
    d_ff = wup_ref.shape[1]
    for c0 in range(0, d_ff, ff_tile):
        u = jnp.maximum(_dot(h, wup_ref[:, c0:c0 + ff_tile]), 0.0)
        acc = acc + _dot((u * u).astype(BF16), wdn_ref[c0:c0 + ff_tile, :])
    if final:
        acc = _rmsnorm(acc, fnw_ref[...])
    o_ref[...] = acc


def _proj_mlp(x, a, wo, nw, wup, wdn, fnw, final, tm=1024):
    t, d = x.shape
    d_ff = wup.shape[1]
    const = lambda i: (0, 0)
    once = pl.Buffered(1)
    return pl.pallas_call(
        functools.partial(_proj_mlp_kernel, final=final, ff_tile=min(1024, d_ff)),
        grid=(t // tm,),
        in_specs=[pl.BlockSpec((tm, d), lambda i: (i, 0)),
                  pl.BlockSpec((tm, a.shape[1]), lambda i: (i, 0)),
                  pl.BlockSpec(wo.shape, const, pipeline_mode=once),
                  pl.BlockSpec((1, d), const),
                  pl.BlockSpec(wup.shape, const, pipeline_mode=once),
                  pl.BlockSpec(wdn.shape, const, pipeline_mode=once),
                  pl.BlockSpec((1, d), const)],
        out_specs=pl.BlockSpec((tm, d), lambda i: (i, 0)),
        out_shape=jax.ShapeDtypeStruct((t, d), F32),
        compiler_params=_params(1),
        name="proj_mlp",
    )(x, a, wo, nw.reshape(1, d), wup, wdn, fnw.reshape(1, d))


def _kv_proj_kernel(x_ref, nw_ref, wc_ref, wk_ref, wvt_ref, kc_ref, vc_ref, ks_ref, kw_ref, vst_ref, vwt_ref):
    i = pl.program_id(0)
    tm = x_ref.shape[0]
    gd = NSA_KV_HEADS * NSA_DH
    h = _rmsnorm(x_ref[...], nw_ref[...]).astype(BF16)
    cmp_in = _dot(h, wc_ref[...])
    for g in range(NSA_KV_HEADS):
        kc_ref[g] = cmp_in[:, g * NSA_DH:(g + 1) * NSA_DH]
        vc_ref[g] = cmp_in[:, gd + g * NSA_DH:gd + (g + 1) * NSA_DH]
    kk = _dot(h, wk_ref[...])
    row = lax.broadcasted_iota(jnp.int32, (tm, AUG), 0)
    lane = lax.broadcasted_iota(jnp.int32, (tm, AUG), 1)
    tok = i * tm + row
    tile = (tok // KEY_TILE).astype(F32)
    offs = (tok % KEY_TILE).astype(F32)
    consts = jnp.where((lane >= NSA_DH) & (lane < NSA_DH + 3), tile,
                       jnp.where((lane >= NSA_DH + 3) & (lane < NSA_DH + 6), 1.0,
                                 jnp.where((lane >= NSA_DH + 6) & (lane < NSA_DH + 9), offs, 0.0)))
    for g in range(NSA_KV_HEADS):
        ks_ref[g] = (kk[:, g * AUG:(g + 1) * AUG] + consts).astype(BF16)
        kw_ref[g] = (kk[:, (NSA_KV_HEADS + g) * AUG:(NSA_KV_HEADS + g + 1) * AUG] + consts).astype(BF16)
    vt = _nt(wvt_ref[...], h)
    r16 = lax.broadcasted_iota(jnp.int32, (VROWS - NSA_DH, tm), 0)
    ones_blk = jnp.where(r16 == 0, 1.0, 0.0).astype(BF16)
    for g in range(NSA_KV_HEADS):
        vst_ref[g, 0:NSA_DH, :] = vt[g * NSA_DH:(g + 1) * NSA_DH, :].astype(BF16)
        vst_ref[g, NSA_DH:VROWS, :] = ones_blk
        vwt_ref[g, 0:NSA_DH, :] = vt[gd + g * NSA_DH:gd + (g + 1) * NSA_DH, :].astype(BF16)
        vwt_ref[g, NSA_DH:VROWS, :] = ones_blk


def _kv_proj(x, nw, wc, wk, wvt, tm=512):
    t, d = x.shape
    g = NSA_KV_HEADS
    const = lambda i: (0, 0)
    rows = lambda i: (0, i, 0)
    cols = lambda i: (0, 0, i)
    return pl.pallas_call(
        _kv_proj_kernel,
        grid=(t // tm,),
        in_specs=[pl.BlockSpec((tm, d), lambda i: (i, 0)),
                  pl.BlockSpec((1, d), const),
                  pl.BlockSpec(wc.shape, const),
                  pl.BlockSpec(wk.shape, const),
                  pl.BlockSpec(wvt.shape, const)],
        out_specs=[pl.BlockSpec((g, tm, NSA_DH), rows), pl.BlockSpec((g, tm, NSA_DH), rows),
                   pl.BlockSpec((g, tm, AUG), rows), pl.BlockSpec((g, tm, AUG), rows),
                   pl.BlockSpec((g, VROWS, tm), cols), pl.BlockSpec((g, VROWS, tm), cols)],
        out_shape=[jax.ShapeDtypeStruct((g, t, NSA_DH), F32), jax.ShapeDtypeStruct((g, t, NSA_DH), F32),
                   jax.ShapeDtypeStruct((g, t, AUG), BF16), jax.ShapeDtypeStruct((g, t, AUG), BF16),
                   jax.ShapeDtypeStruct((g, VROWS, t), BF16), jax.ShapeDtypeStruct((g, VROWS, t), BF16)],
        compiler_params=_params(1),
        name="kv_proj",
    )(x, nw.reshape(1, d), wc, wk, wvt)


def _nsa_inproj_kernel(x_ref, nw_ref, wt_ref, sl_ref, q_ref, gl_ref, *, nqb):
    i = pl.program_id(0)
    qw = NSA_HEADS * NSA_DH
    h = _rmsnorm(x_ref[...], nw_ref[...]).astype(BF16)
    pt = _nt(wt_ref[...], h)
    r16 = lax.broadcasted_iota(jnp.int32, (16, QL), 0)
    ql = (lax.broadcasted_iota(jnp.int32, (1, QL), 1) & (Q_BLOCK - 1)).astype(F32)

    def split3(v):
        a = v.astype(BF16).astype(F32)
        b = (v - a).astype(BF16).astype(F32)
        return a, b, v - a - b

    for g in range(NSA_KV_HEADS):
        slope = sl_ref[g]
        s1, s2, s3 = split3(slope * float(KEY_TILE))
        r1, r2, r3 = split3(slope)
        for qb in range(nqb):
            lanes = slice(qb * Q_BLOCK, (qb + 1) * Q_BLOCK)
            for j in range(NSA_GROUP):
                r0 = (g * NSA_GROUP + j) * NSA_DH
                q_ref[g, qb, 0:NSA_DH, j * Q_BLOCK:(j + 1) * Q_BLOCK] = pt[r0:r0 + NSA_DH, lanes].astype(BF16)
            tq = ((i * nqb + qb) * Q_BLOCK).astype(F32) + ql
            o1, o2, o3 = split3(-(slope * tq))
            aug = jnp.zeros((16, QL), F32)
            for k, term in enumerate((s1, s2, s3, o1, o2, o3, r1, r2, r3)):
                aug = jnp.where(r16 == k, term, aug)
            q_ref[g, qb, NSA_DH:NSA_DH + 16, :] = aug.astype(BF16)
            q_ref[g, qb, NSA_DH + 16:AUG, :] = jnp.zeros((AUG - NSA_DH - 16, QL), BF16)
            for c in range(3):
                for j in range(NSA_GROUP):
                    r0 = qw + (g * 3 + c) * NSA_GROUP + j
                    gl_ref[g, qb, c:c + 1, j * Q_BLOCK:(j + 1) * Q_BLOCK] = pt[r0:r0 + 1, lanes]


def _nsa_inproj(x, nw, wt, slopes, nqb=4):
    t, d = x.shape
    nq = t // Q_BLOCK
    g = NSA_KV_HEADS
    tm = nqb * Q_BLOCK
    return pl.pallas_call(
        functools.partial(_nsa_inproj_kernel, nqb=nqb),
        grid=(t // tm,),
        in_specs=[pl.BlockSpec((tm, d), lambda i: (i, 0)),
                  pl.BlockSpec((1, d), lambda i: (0, 0)),
                  pl.BlockSpec(wt.shape, lambda i: (0, 0)),
                  pl.BlockSpec(slopes.shape, lambda i: (0, 0, 0))],
        out_specs=[pl.BlockSpec((g, nqb, AUG, QL), lambda i: (0, i, 0, 0)),
                   pl.BlockSpec((g, nqb, 3, QL), lambda i: (0, i, 0, 0))],
        out_shape=[jax.ShapeDtypeStruct((g, nq, AUG, QL), BF16),
                   jax.ShapeDtypeStruct((g, nq, 3, QL), F32)],
        compiler_params=_params(1),
        name="nsa_inproj",
    )(x, nw.reshape(1, d), wt, slopes)


def _compress_kernel(hf_ref, pe_ref, w1_ref, w2_ref, o_ref):
    x = hf_ref[0, 0]
    nc, half = x.shape
    pe_lo = pe_ref[0, 0:1, :]
    pe_hi = pe_ref[0, 1:2, :]
    w1_lo = w1_ref[0, 0:half, :]
    w1_hi = w1_ref[0, half:2 * half, :]
    u = _dot((x + pe_lo).astype(BF16), w1_lo)
    vv = _dot((x + pe_hi).astype(BF16), w1_hi)
    nxt = pltpu.roll(vv, nc - 1, 0)
    pad = _dot(jnp.broadcast_to(pe_hi, (8, half)).astype(BF16), w1_hi)[0:1]
    row = lax.broadcasted_iota(jnp.int32, vv.shape, 0)
    pre = u + jnp.where(row == nc - 1, pad, nxt)
    hid = 0.5 * pre * (1.0 + jnp.tanh(0.7978845608028654 * (pre + 0.044715 * (pre * pre * pre))))
    o_ref[0, 0] = _dot(hid.astype(BF16), w2_ref[0])


def _compress(hf, pe, w1, w2):
    s, g, nc, half = hf.shape
    return pl.pallas_call(
        _compress_kernel,
        grid=(s, g),
        in_specs=[pl.BlockSpec((1, 1, nc, half), lambda i, j: (i, j, 0, 0)),
                  pl.BlockSpec((1, 2, half), lambda i, j: (i, 0, 0)),
                  pl.BlockSpec((1,) + w1.shape[1:], lambda i, j: (i, 0, 0)),
                  pl.BlockSpec((1,) + w2.shape[1:], lambda i, j: (i, 0, 0))],
        out_specs=pl.BlockSpec((1, 1, nc, w2.shape[2]), lambda i, j: (i, j, 0, 0)),
        out_shape=jax.ShapeDtypeStruct((s, g, nc, w2.shape[2]), F32),
        compiler_params=_params(2),
        name="compress",
    )(hf, pe, w1, w2)


def _nsa_cmp_kernel(q_ref, kc_ref, vct_ref, cb_ref, oc_ref, sel_ref, l_ref, p_ref, *, ns, topk, nqb):
    ratio = SLC_BLOCK // CMP_STRIDE
    qis = [pl.program_id(1) * nqb + qb for qb in range(nqb)]

    def attend(nv):
        cb = cb_ref[0:nv, :]
        n_i = lax.broadcasted_iota(jnp.int32, (nv, Q_BLOCK), 0)
        ql = lax.broadcasted_iota(jnp.int32, (nv, Q_BLOCK), 1)
        n_f = n_i.astype(F32)
        mxs = []
        for qb in range(nqb):
            qa = q_ref[0, qb]
            t0f = (qis[qb] * Q_BLOCK).astype(F32)
            mx = jnp.full((1, QL), NEG_INF, F32)
            for u in range(ratio):
                s = _dot(kc_ref[0, u * ns:u * ns + nv, :], qa)
                s = jnp.where(cb >= float(CMP_STRIDE * u + CMP_BLOCK - 1) - t0f, s, NEG_INF)
                p_ref[qb, u * ns:u * ns + nv, :] = s
                mx = jnp.maximum(mx, jnp.max(s, axis=0, keepdims=True))
            mxs.append(mx)
        cands, forceds = [], []
        for qb in range(nqb):
            mx = mxs[qb]
            valid = mx > 0.5 * NEG_INF
            acc = jnp.zeros((VROWS, QL), F32)
            head_pool = [None] * NSA_GROUP
            for u in range(ratio):
                p = jnp.exp(p_ref[qb, u * ns:u * ns + nv, :] - mx)
                acc = acc + _dot(vct_ref[0, :, u * ns:u * ns + nv], p.astype(BF16))
                for j in range(NSA_GROUP):
                    pj = p[:, j * Q_BLOCK:(j + 1) * Q_BLOCK]
                    head_pool[j] = pj if head_pool[j] is None else head_pool[j] + pj
                    if u == ratio - 1:
                        head_pool[j] = head_pool[j] + jnp.where(n_i >= 1, pltpu.roll(pj, 1, 0), 0.0)
            inv = jnp.where(valid, 1.0 / acc[NSA_DH:NSA_DH + 1], 0.0)
            oc_ref[0, qb] = acc[0:NSA_DH] * inv
            pooled = head_pool[0] * inv[:, 0:Q_BLOCK]
            for j in range(1, NSA_GROUP):
                pooled = pooled + head_pool[j] * inv[:, j * Q_BLOCK:(j + 1) * Q_BLOCK]
            cur = 2 * qis[qb] + (ql >= SLC_BLOCK).astype(jnp.int32)
            ok = n_i <= cur
            forced = ok & ((n_i == 0) | (n_i == cur) | (n_i == cur - 1))
            forceds.append(forced)
            cands.append(jnp.where(ok & jnp.logical_not(forced), pooled, NEG_INF))
        scores = list(cands)
        for _ in range(topk - 3):
            for qb in range(nqb):
                best = jnp.max(scores[qb], axis=0, keepdims=True)
                idx = jnp.min(jnp.where(scores[qb] == best, n_f, float(ns)), axis=0, keepdims=True)
                scores[qb] = jnp.where(n_f == idx, -jnp.inf, scores[qb])
        nt = ns // 2
        pair = jnp.where((lax.broadcasted_iota(jnp.int32, (nv, 128), 0) >> 1)
                         == lax.broadcasted_iota(jnp.int32, (nv, 128), 1), 1.0, 0.0)
        lane = lax.broadcasted_iota(jnp.int32, (1, 128), 1)
        before = (lax.broadcasted_iota(jnp.int32, (128, 128), 0)
                  < lax.broadcasted_iota(jnp.int32, (128, 128), 1)).astype(BF16)
        k_f = lax.broadcasted_iota(jnp.int32, (nt, 128), 0).astype(F32)
        for qb in range(nqb):
            picked = (scores[qb] == -jnp.inf) & (cands[qb] > 0.5 * NEG_INF)
            sel = jnp.where(forceds[qb] | picked, 1.0, 0.0)
            sel_ref[0, qb, 0:nv, :] = sel
            if nv < ns:
                sel_ref[0, qb, nv:ns, :] = jnp.zeros((ns - nv, Q_BLOCK), F32)
            any_q = jnp.max(sel, axis=1, keepdims=True)
            active = (jnp.sum(any_q * pair, axis=0, keepdims=True) > 0.5) & (lane < qis[qb])
            act = jnp.where(active, 1.0, 0.0)
            slot = _dot(jnp.broadcast_to(act, (8, 128)).astype(BF16), before)[0:1]
            hit = jnp.where((k_f == slot) & active, lane.astype(F32), 0.0)
            lst = jnp.sum(hit, axis=1, keepdims=True)
            cnt = jnp.sum(act, axis=1, keepdims=True)
            lst = jnp.where(lax.broadcasted_iota(jnp.int32, (nt, 1), 0) == nt - 1, cnt, lst)
            l_ref[0, qb] = lst.astype(jnp.int32)

    n_cls = 8
    step = ns // n_cls
    need = 2 * qis[-1] + 2
    for k in range(n_cls):
        lo, hi = k * step, (k + 1) * step

        @pl.when((need > lo) & (need <= hi))
        def _():
            attend(hi)


def _nsa_cmp(qa, kc, vct, cbase):
    g, nq, _, _ = qa.shape
    dh = NSA_DH
    ncmp = kc.shape[1]
    ns = ncmp // (SLC_BLOCK // CMP_STRIDE)
    topk = min(SLC_TOPK, ns)
    nqb = CMP_QBLOCKS
    return pl.pallas_call(
        functools.partial(_nsa_cmp_kernel, ns=ns, topk=topk, nqb=nqb),
        grid=(g, nq // nqb),
        in_specs=[pl.BlockSpec((1, nqb, AUG, QL), lambda i, j: (i, j, 0, 0)),
                  pl.BlockSpec((1, ncmp, AUG), lambda i, j: (i, 0, 0)),
                  pl.BlockSpec((1, VROWS, ncmp), lambda i, j: (i, 0, 0)),
                  pl.BlockSpec((ns, QL), lambda i, j: (0, 0))],
        out_specs=[pl.BlockSpec((1, nqb, dh, QL), lambda i, j: (i, j, 0, 0)),
                   pl.BlockSpec((1, nqb, ns, Q_BLOCK), lambda i, j: (i, j, 0, 0)),
                   pl.BlockSpec((1, nqb, ns // 2, 1), lambda i, j: (i, j, 0, 0))],
        out_shape=[jax.ShapeDtypeStruct((g, nq, dh, QL), F32),
                   jax.ShapeDtypeStruct((g, nq, ns, Q_BLOCK), F32),
                   jax.ShapeDtypeStruct((g, nq, ns // 2, 1), jnp.int32)],
        scratch_shapes=[pltpu.VMEM((nqb, ncmp, QL), F32)],
        compiler_params=_params(2),
        name="nsa_cmp",
    )(qa, kc, vct, cbase)


def _nsa_sw_kernel(lists_ref, q_ref, gl_ref, oc_ref, sel_ref, base_ref, ks_ref, vst_ref,
                   kw_ref, vwt_ref, o_ref, m_ref, acc_ref, *, nq, nqb):
    g = pl.program_id(0)
    step = pl.program_id(1)
    half = SLC_BLOCK
    tile_base = base_ref[...]

    def probs(s, mx):
        return jnp.exp((s - mx).astype(BF16))

    def sel_mask(ra, rb):
        mk = jnp.concatenate([jnp.broadcast_to(ra, (half, Q_BLOCK)), jnp.broadcast_to(rb, (half, Q_BLOCK))], axis=0)
        return jnp.concatenate([mk] * NSA_GROUP, axis=1) > 0.5

    def listed_tile(qb, cnt, slot):
        live = slot < cnt
        m = jnp.where(live, lists_ref[(g * nq + qis[qb]) * nq + jnp.minimum(slot, nq - 2)], 0)
        keep = jnp.where(live, 1.0, 0.0)
        pos = pl.multiple_of(m * KEY_TILE, KEY_TILE)
        mask = sel_mask(sel_ref[0, qb, pl.ds(2 * m, 1), :] * keep, sel_ref[0, qb, pl.ds(2 * m + 1, 1), :] * keep)
        return ks_ref[0, pl.ds(pos, KEY_TILE), :], vst_ref[0, :, pl.ds(pos, KEY_TILE)], mask

    def masked_scores(qa, k_tiles, masks):
        sc = _dot(jnp.concatenate(k_tiles, axis=0), qa)
        tiles, mx = [], None
        for u, mask in enumerate(masks):
            s = jnp.where(mask, sc[u * KEY_TILE:(u + 1) * KEY_TILE, :], NEG_INF)
            mu = jnp.max(s, axis=0, keepdims=True)
            mx = mu if mx is None else jnp.maximum(mx, mu)
            tiles.append(s)
        return tiles, mx

    def pv(s_tiles, v_tiles, mx):
        return _dot(jnp.concatenate(v_tiles, axis=1), jnp.concatenate([probs(s, mx) for s in s_tiles], axis=0))

    qis = [step * nqb + qb for qb in range(nqb)]
    cnts = [lists_ref[(g * nq + qi) * nq + nq - 1] for qi in qis]

    n_wt = (WINDOW + Q_BLOCK) // KEY_TILE
    staged = []
    for qb in range(nqb):
        qi, cnt = qis[qb], cnts[qb]
        t0 = qi * Q_BLOCK
        qa = q_ref[0, qb]
        wk, wv, wmask = [], [], []
        for u in range(n_wt):
            pos_raw = t0 - WINDOW + u * KEY_TILE
            pos = pl.multiple_of(jnp.maximum(pos_raw, 0), KEY_TILE)
            wk.append(kw_ref[0, pl.ds(pos, KEY_TILE), :])
            wv.append(vwt_ref[0, :, pl.ds(pos, KEY_TILE)])
            live = pos_raw >= 0
            if u == 0:
                wmask.append(jnp.logical_and(tile_base < 0, live))
            elif u == n_wt - 1:
                wmask.append(tile_base >= 0)
            else:
                wmask.append(live)
        posd = pl.multiple_of(t0, KEY_TILE)
        diag_mask = sel_mask(sel_ref[0, qb, pl.ds(2 * qi, 1), :],
                             sel_ref[0, qb, pl.ds(2 * qi + 1, 1), :]) & (tile_base >= 0)
        listed = [listed_tile(qb, cnt, jnp.int32(u)) for u in range(SEL_STATIC)]
        stiles, m0 = masked_scores(qa, [ks_ref[0, pl.ds(posd, KEY_TILE), :]] + [t[0] for t in listed],
                                   [diag_mask] + [t[2] for t in listed])
        wtiles, mw = masked_scores(qa, wk, wmask)
        staged.append((stiles, m0, [vst_ref[0, :, pl.ds(posd, KEY_TILE)]] + [t[1] for t in listed], wtiles, mw, wv))

    o_wins = []
    for qb in range(nqb):
        stiles, m0, svals, wtiles, mw, wv = staged[qb]
        m_ref[qb] = m0
        acc_ref[qb] = pv(stiles, svals, m0)
        accw = pv(wtiles, wv, mw)
        o_wins.append(accw[0:NSA_DH] * (1.0 / accw[NSA_DH:NSA_DH + 1]))

    def chunk_body(ci, carry):
        staged = []
        for qb in range(nqb):
            listed = [listed_tile(qb, cnts[qb], SEL_STATIC + ci * SEL_CHUNK + u) for u in range(SEL_CHUNK)]
            tiles, cmax = masked_scores(q_ref[0, qb], [t[0] for t in listed], [t[2] for t in listed])
            staged.append((tiles, cmax, [t[1] for t in listed]))
        for qb in range(nqb):
            tiles, cmax, vals = staged[qb]
            m_old = m_ref[qb]
            m_new = jnp.maximum(m_old, cmax)
            acc_ref[qb] = jnp.exp(m_old - m_new) * acc_ref[qb] + pv(tiles, vals, m_new)
            m_ref[qb] = m_new
        return carry

    most = cnts[0]
    for cnt in cnts[1:]:
        most = jnp.maximum(most, cnt)
    lax.fori_loop(0, (jnp.maximum(most - SEL_STATIC, 0) + SEL_CHUNK - 1) // SEL_CHUNK, chunk_body, 0)

    for qb in range(nqb):
        o_sel = acc_ref[qb, 0:NSA_DH, :] * (1.0 / acc_ref[qb, NSA_DH:NSA_DH + 1, :])

        gl = gl_ref[0, qb]
        gate = 1.0 / (1.0 + jnp.exp(-gl))
        out_t = gate[0:1] * oc_ref[0, qb] + gate[1:2] * o_sel + gate[2:3] * o_wins[qb]
        halves = []
        for j in range(0, NSA_GROUP, 2):
            pair = jnp.concatenate([out_t[:, j * Q_BLOCK:(j + 1) * Q_BLOCK],
                                    out_t[:, (j + 1) * Q_BLOCK:(j + 2) * Q_BLOCK]], axis=0)
            halves.append(pair.T)
        o_ref[qb * Q_BLOCK:(qb + 1) * Q_BLOCK, :] = jnp.concatenate(halves, axis=1).astype(o_ref.dtype)


def _nsa_sw(lists, qa, glog, oct_, selt, base, ks, vst, kw, vwt):
    g, nq, _, _ = qa.shape
    dh = NSA_DH
    t = ks.shape[1]
    ns = selt.shape[2]
    assert lists.shape[0] == g * nq * nq
    nqb = SW_QBLOCKS
    grid_spec = pltpu.PrefetchScalarGridSpec(
        num_scalar_prefetch=1,
        grid=(g, nq // nqb),
        in_specs=[pl.BlockSpec((1, nqb, AUG, QL), lambda i, j, w: (i, j, 0, 0)),
                  pl.BlockSpec((1, nqb, 3, QL), lambda i, j, w: (i, j, 0, 0)),
                  pl.BlockSpec((1, nqb, dh, QL), lambda i, j, w: (i, j, 0, 0)),
                  pl.BlockSpec((1, nqb, ns, Q_BLOCK), lambda i, j, w: (i, j, 0, 0)),
                  pl.BlockSpec((KEY_TILE, QL), lambda i, j, w: (0, 0)),
                  pl.BlockSpec((1, t, AUG), lambda i, j, w: (i, 0, 0)),
                  pl.BlockSpec((1, VROWS, t), lambda i, j, w: (i, 0, 0)),
                  pl.BlockSpec((1, t, AUG), lambda i, j, w: (i, 0, 0)),
                  pl.BlockSpec((1, VROWS, t), lambda i, j, w: (i, 0, 0))],
        out_specs=pl.BlockSpec((nqb * Q_BLOCK, NSA_GROUP * dh), lambda i, j, w: (j, i)),
        scratch_shapes=[pltpu.VMEM((nqb, 1, QL), F32), pltpu.VMEM((nqb, VROWS, QL), F32)],
    )
    return pl.pallas_call(
        functools.partial(_nsa_sw_kernel, nq=nq, nqb=nqb),
        grid_spec=grid_spec,
        out_shape=jax.ShapeDtypeStruct((t, g * NSA_GROUP * dh), BF16),
        compiler_params=_params(2),
        name="nsa_sw",
    )(lists, qa, glog, oct_, selt, base, ks, vst, kw, vwt)


def kernel(x, a_norm_w, a_w_in, a_gnorm_w, a_w_out, a_lower_bounds, kv_norm_w, kv_w, cmp_pe_k, cmp_w1_k,
           cmp_w2_k, cmp_pe_v, cmp_w1_v, cmp_w2_v, b_norm_w, b_w_in, b_w_out, mlp_norm_w, mlp_w_up,
           mlp_w_down, final_norm_w):
    bsz, t, d = x.shape
    assert bsz == 1 and t % Q_BLOCK == 0 and t >= WINDOW + Q_BLOCK
    n_a = a_w_in.shape[0]
    n_b = b_w_in.shape[0]
    nq = t // Q_BLOCK
    xs = x[0]

    for layer in range(n_a):
        o = _hgrn(xs, a_norm_w[layer], a_w_in[layer].astype(BF16), a_lower_bounds, a_gnorm_w[layer], layer)
        xs = _proj_mlp(xs, o, a_w_out[layer].astype(BF16), mlp_norm_w[layer],
                       mlp_w_up[layer].astype(BF16), mlp_w_down[layer].astype(BF16),
                       final_norm_w, final=False)

    gd = NSA_KV_HEADS * NSA_DH
    kvw = kv_w.reshape(d, N_KV_STREAMS, NSA_KV_HEADS, NSA_DH)
    wc = kvw[:, 0:2].reshape(d, 2 * gd).astype(BF16)
    wk = jnp.pad(jnp.stack([kvw[:, 2], kvw[:, 4]], axis=1), ((0, 0), (0, 0), (0, 0), (0, AUG - NSA_DH)))
    wk = wk.reshape(d, 2 * NSA_KV_HEADS * AUG).astype(BF16)
    wvt = jnp.stack([kvw[:, 3], kvw[:, 5]], axis=1).reshape(d, 2 * gd).T.astype(BF16)
    kc_in, vc_in, ks, kw, vst, vwt = _kv_proj(xs, kv_norm_w, wc, wk, wvt)

    ncmp = t // CMP_STRIDE
    half = CMP_STRIDE * NSA_DH
    hf = jnp.stack([kc_in.reshape(NSA_KV_HEADS, ncmp, half), vc_in.reshape(NSA_KV_HEADS, ncmp, half)])
    pe = jnp.stack([cmp_pe_k.reshape(2, half), cmp_pe_v.reshape(2, half)])
    w1 = jnp.stack([cmp_w1_k, cmp_w1_v]).astype(BF16)
    w2 = jnp.stack([cmp_w2_k, cmp_w2_v]).astype(BF16)
    cmp_out = _compress(hf, pe, w1, w2)
    ratio = SLC_BLOCK // CMP_STRIDE
    ns = ncmp // ratio
    perm = cmp_out.reshape(2, NSA_KV_HEADS, ns, ratio, NSA_DH).transpose(0, 1, 3, 2, 4).reshape(
        2, NSA_KV_HEADS, ncmp, NSA_DH)
    crow = jnp.arange(ncmp)
    cmp_end = SLC_BLOCK * (crow % ns) + CMP_STRIDE * (crow // ns) + (CMP_BLOCK - 1)
    aug_cols = jnp.stack([cmp_end // KEY_TILE] * 3 + [jnp.ones_like(cmp_end)] * 3 + [cmp_end % KEY_TILE] * 3,
                         axis=1).astype(F32)
    aug_cols = jnp.pad(aug_cols, ((0, 0), (0, AUG - NSA_DH - aug_cols.shape[1])))
    kc = jnp.concatenate([perm[0], jnp.broadcast_to(aug_cols, (NSA_KV_HEADS, ncmp, AUG - NSA_DH))],
                         axis=2).astype(BF16)
    ones_rows = jnp.zeros((NSA_KV_HEADS, VROWS - NSA_DH, ncmp), F32).at[:, 0, :].set(1.0)
    vct = jnp.concatenate([perm[1].transpose(0, 2, 1), ones_rows], axis=1).astype(BF16)

    slopes = jnp.exp2(-8.0 * jnp.arange(1, NSA_HEADS + 1, dtype=F32) / NSA_HEADS)
    slopes = jnp.repeat(slopes.reshape(NSA_KV_HEADS, 1, NSA_GROUP), Q_BLOCK, axis=2)
    q_off = jnp.tile(jnp.arange(Q_BLOCK, dtype=F32), NSA_GROUP)[None, :]
    base = q_off - jnp.arange(KEY_TILE, dtype=F32)[:, None]
    cbase = q_off - float(SLC_BLOCK) * jnp.arange(ns, dtype=F32)[:, None]

    qw = NSA_HEADS * NSA_DH
    n_gate = b_w_in.shape[2] - qw
    pad_rows = (-(qw + n_gate)) % 128
    for b in range(n_b):
        w_q = b_w_in[b][:, 0:qw] * (NSA_DH ** -0.5)
        w_g = b_w_in[b][:, qw:].reshape(d, NSA_KV_HEADS, NSA_GROUP, 3).transpose(0, 1, 3, 2).reshape(d, n_gate)
        wt_in = jnp.pad(jnp.concatenate([w_q, w_g], axis=1), ((0, 0), (0, pad_rows))).T.astype(BF16)
        qa, glog = _nsa_inproj(xs, b_norm_w[b], wt_in, slopes)
        oct_, selt, lists = _nsa_cmp(qa, kc, vct, cbase)
        o = _nsa_sw(lists.reshape(-1), qa, glog, oct_, selt, base, ks, vst, kw, vwt)
        xs = _proj_mlp(xs, o, b_w_out[b].astype(BF16), mlp_norm_w[n_a + b],
                       mlp_w_up[n_a + b].astype(BF16), mlp_w_down[n_a + b].astype(BF16),
                       final_norm_w, final=(b == n_b - 1))
    return xs[None]
```
